```python
import math
import jax
import jax.numpy as jnp
from jax import lax
import numpy as np

D_MODEL = 1024
BATCH = 4
SEQ = 4096
DEPTH = 4
DEC_BATCH = 128
DEC_SEQ = 1
PAST_LEN = 2048
PAGE_SIZE = 128

N_SSD_LAYERS = DEPTH // 2
N_ATTN_LAYERS = DEPTH - N_SSD_LAYERS

SSD_EXPAND = 2
SSD_D_INNER = SSD_EXPAND * D_MODEL
SSD_HEAD_DIM = 64
SSD_N_HEADS = SSD_D_INNER // SSD_HEAD_DIM
SSD_N_GROUPS = 4
SSD_HEADS_PER_GROUP = SSD_N_HEADS // SSD_N_GROUPS
SSD_D_STATE = 128
SSD_D_CONV = 4
SSD_CHUNK = 128
SSD_GN = SSD_N_GROUPS * SSD_D_STATE
SSD_CONV_DIM = SSD_D_INNER + 2 * SSD_GN
SSD_IN_DIM = SSD_D_INNER + SSD_CONV_DIM + SSD_N_HEADS

ATTN_N_HEADS = 8
ATTN_HEAD_DIM = D_MODEL // ATTN_N_HEADS // 2
ATTN_V_DIM = 2 * ATTN_HEAD_DIM
ATTN_QK_DIM = ATTN_N_HEADS * 2 * ATTN_HEAD_DIM
ATTN_VO_DIM = ATTN_N_HEADS * ATTN_V_DIM
ROT_DIM = ATTN_HEAD_DIM // 4
ROPE_THETA = 500000.0
Q_BLOCK = 128

D_FF = 4 * D_MODEL
RMS_EPS = 1e-6

kernel_name = 'yoco_mamba2_diffattn_decode_step'


def rmsnorm(x, g):
    xf = x.astype(jnp.float32)
    y = xf * lax.rsqrt(jnp.mean(xf * xf, axis=-1, keepdims=True) + RMS_EPS)
    return (y * g.astype(jnp.float32)).astype(x.dtype)


def squared_relu_mlp(x, w_up, w_down):
    h = jax.nn.relu(x @ w_up)
    return (h * h) @ w_down


def ssd_in_proj(xn, w_in):
    zxbcdt = xn @ w_in
    z = zxbcdt[..., :SSD_D_INNER]
    xbc = zxbcdt[..., SSD_D_INNER:SSD_D_INNER + SSD_CONV_DIM]
    dt_raw = zxbcdt[..., SSD_D_INNER + SSD_CONV_DIM:]
    return z, xbc, dt_raw


def causal_dwconv(xbc, buf, w, b):
    length = xbc.shape[1]
    xpad = jnp.concatenate([buf.astype(xbc.dtype), xbc], axis=1)
    acc = b + xpad[:, 0:length] * w[0]
    for k in range(1, SSD_D_CONV):
        acc = acc + xpad[:, k:k + length] * w[k]
    return jax.nn.silu(acc), xpad[:, -(SSD_D_CONV - 1):]


def split_xbc(xbc):
    bsz, length = xbc.shape[:2]
    x = xbc[..., :SSD_D_INNER].reshape(bsz, length, SSD_N_HEADS, SSD_HEAD_DIM)
    b_in = xbc[..., SSD_D_INNER:SSD_D_INNER + SSD_GN].reshape(bsz, length, SSD_N_GROUPS, SSD_D_STATE)
    c_in = xbc[..., SSD_D_INNER + SSD_GN:].reshape(bsz, length, SSD_N_GROUPS, SSD_D_STATE)
    return x, b_in, c_in


def ssd_chunked(x, dt, a, b_in, c_in):
    bsz, seqlen = x.shape[:2]
    nc = seqlen // SSD_CHUNK
    G, R, P, N, L = SSD_N_GROUPS, SSD_HEADS_PER_GROUP, SSD_HEAD_DIM, SSD_D_STATE, SSD_CHUNK
    xg = (x.astype(jnp.float32) * dt[..., None]).reshape(bsz, nc, L, G, R, P)
    adt = jnp.moveaxis((a * dt).reshape(bsz, nc, L, G, R), 2, -1)
    a_cum = jnp.cumsum(adt, axis=-1)
    bc = b_in.astype(jnp.float32).reshape(bsz, nc, L, G, N)
    cc = c_in.astype(jnp.float32).reshape(bsz, nc, L, G, N)
    causal = jnp.tril(jnp.ones((L, L), dtype=bool))
    seg = a_cum[..., :, None] - a_cum[..., None, :]
    decay = jnp.exp(jnp.where(causal, seg, -jnp.inf))
    cb = jnp.einsum('bclgn,bcsgn->bcgls', cc, bc)
    y_diag = jnp.einsum('bcgls,bcgrls,bcsgrp->bclgrp', cb, decay, xg)
    decay_end = jnp.exp(a_cum[..., -1:] - a_cum)
    states = jnp.einsum('bcsgn,bcgrs,bcsgrp->bcgrpn', bc, decay_end, xg)
    chunk_decay = jnp.exp(a_cum[..., -1])

    def pass_state(h, inp):
        st, dec = inp
        return h * dec[..., None, None] + st, h

    h0 = jnp.zeros((bsz, G, R, P, N), jnp.float32)
    h_final, h_prev = lax.scan(pass_state, h0, (jnp.moveaxis(states, 1, 0), jnp.moveaxis(chunk_decay, 1, 0)))
    h_prev = jnp.moveaxis(h_prev, 0, 1)
    y_off = jnp.einsum('bclgn,bcgrpn,bcgrl->bclgrp', cc, h_prev, jnp.exp(a_cum))
    y = (y_diag + y_off).reshape(bsz, seqlen, SSD_N_HEADS, SSD_HEAD_DIM)
    return y, h_final.reshape(bsz, SSD_N_HEADS, SSD_HEAD_DIM, SSD_D_STATE)


def ssd_scan_tokens(x, dt, a, b_in, c_in, h0):
    bsz = x.shape[0]
    G, R, P, N = SSD_N_GROUPS, SSD_HEADS_PER_GROUP, SSD_HEAD_DIM, SSD_D_STATE
    a_gr = a.reshape(G, R)

    def step(h, inp):
        xt, dtt, bt, ct = inp
        xt = xt.astype(jnp.float32).reshape(bsz, G, R, P)
        dtt = dtt.reshape(bsz, G, R)
        h = h * jnp.exp(a_gr * dtt)[..., None, None] + jnp.einsum('bgrp,bgn->bgrpn', xt * dtt[..., None], bt.astype(jnp.float32))
        yt = jnp.einsum('bgrpn,bgn->bgrp', h, ct.astype(jnp.float32))
        return h, yt.reshape(bsz, SSD_N_HEADS, P)

    h_init = h0.astype(jnp.float32).reshape(bsz, G, R, P, N)
    xs = (jnp.moveaxis(x, 1, 0), jnp.moveaxis(dt, 1, 0), jnp.moveaxis(b_in, 1, 0), jnp.moveaxis(c_in, 1, 0))
    h_last, ys = lax.scan(step, h_init, xs)
    return jnp.moveaxis(ys, 0, 1), h_last.reshape(bsz, SSD_N_HEADS, P, N)


def ssd_out(y, x, z, d, gate_norm, w_out, dtype):
    bsz, length = y.shape[:2]
    y = y + x.astype(jnp.float32) * d.astype(jnp.float32)[:, None]
    g = y.reshape(bsz, length, SSD_D_INNER) * jax.nn.silu(z.astype(jnp.float32))
    g = g.reshape(bsz, length, SSD_N_GROUPS, SSD_D_INNER // SSD_N_GROUPS)
    g = g * lax.rsqrt(jnp.mean(g * g, axis=-1, keepdims=True) + RMS_EPS)
    g = g.reshape(bsz, length, SSD_D_INNER) * gate_norm.astype(jnp.float32)
    return g.astype(dtype) @ w_out


def ssd_dt_a(dt_raw, dt_bias, a_log):
    dt = jax.nn.softplus(dt_raw.astype(jnp.float32) + dt_bias.astype(jnp.float32))
    a = -jnp.exp(a_log.astype(jnp.float32))
    return dt, a


def ssd_mixer_prompt(xn, w_in, conv_w, conv_b, dt_bias, a_log, d, gate_norm, w_out):
    z, xbc, dt_raw = ssd_in_proj(xn, w_in)
    buf0 = jnp.zeros((xn.shape[0], SSD_D_CONV - 1, SSD_CONV_DIM), xn.dtype)
    xbc, conv_state = causal_dwconv(xbc, buf0, conv_w, conv_b)
    x, b_in, c_in = split_xbc(xbc)
    dt, a = ssd_dt_a(dt_raw, dt_bias, a_log)
    y, ssm_state = ssd_chunked(x, dt, a, b_in, c_in)
    return ssd_out(y, x, z, d, gate_norm, w_out, xn.dtype), ssm_state, conv_state


def ssd_mixer_sample(xn, ssm0, conv0, w_in, conv_w, conv_b, dt_bias, a_log, d, gate_norm, w_out):
    z, xbc, dt_raw = ssd_in_proj(xn, w_in)
    xbc, conv_state = causal_dwconv(xbc, conv0, conv_w, conv_b)
    x, b_in, c_in = split_xbc(xbc)
    dt, a = ssd_dt_a(dt_raw, dt_bias, a_log)
    y, ssm_state = ssd_scan_tokens(x, dt, a, b_in, c_in, ssm0)
    out = ssd_out(y, x, z, d, gate_norm, w_out, xn.dtype)
    return out, ssm_state.astype(ssm0.dtype), conv_state.astype(conv0.dtype)


def rope_partial(t, pos):
    half = ROT_DIM // 2
    inv = ROPE_THETA ** (-jnp.arange(0, ROT_DIM, 2, dtype=jnp.float32) / ROT_DIM)
    ang = pos.astype(jnp.float32)[:, None] * inv[None, :]
    cos = jnp.cos(ang)[:, None, None, :]
    sin = jnp.sin(ang)[:, None, None, :]
    rot = t[..., :ROT_DIM].astype(jnp.float32)
    r1, r2 = rot[..., :half], rot[..., half:]
    rot = jnp.concatenate([r1 * cos - r2 * sin, r2 * cos + r1 * sin], axis=-1)
    return jnp.concatenate([rot.astype(t.dtype), t[..., ROT_DIM:]], axis=-1)


def shared_kv(h, kv_norm, w_kv, pos):
    bsz, length = h.shape[:2]
    kv = rmsnorm(h, kv_norm) @ w_kv
    k = kv[..., :ATTN_QK_DIM].reshape(bsz, length, ATTN_N_HEADS, 2, ATTN_HEAD_DIM)
    v = kv[..., ATTN_QK_DIM:].reshape(bsz, length, ATTN_N_HEADS, ATTN_V_DIM)
    return rope_partial(k, pos), v


def diff_query(xn, w_q, pos):
    bsz, length = xn.shape[:2]
    q = (xn @ w_q).reshape(bsz, length, ATTN_N_HEADS, 2, ATTN_HEAD_DIM)
    return rope_partial(q, pos)


def diff_lambda(lq1, lk1, lq2, lk2, lam_init):
    f = jnp.float32
    return (jnp.exp(jnp.sum(lq1.astype(f) * lk1.astype(f))) - jnp.exp(jnp.sum(lq2.astype(f) * lk2.astype(f))) + lam_init)


def diff_attn_prompt(q, k, v, lam):
    bsz, seqlen = q.shape[:2]
    nb = seqlen // Q_BLOCK
    scale = ATTN_HEAD_DIM ** -0.5
    qb = jnp.moveaxis(q.reshape(bsz, nb, Q_BLOCK, ATTN_N_HEADS, 2, ATTN_HEAD_DIM), 1, 0)
    kpos = jnp.arange(seqlen)

    def one_block(args):
        qi, blk = args
        qpos = blk * Q_BLOCK + jnp.arange(Q_BLOCK)
        s = jnp.einsum('bqhjd,bkhjd->bhjqk', qi, k).astype(jnp.float32) * scale
        s = jnp.where((kpos[None, :] <= qpos[:, None]), s, -jnp.inf)
        p = jax.nn.softmax(s, axis=-1)
        att = p[:, :, 0] - lam * p[:, :, 1]
        return jnp.einsum('bhqk,bkhe->bqhe', att.astype(v.dtype), v)

    o = lax.map(one_block, (qb, jnp.arange(nb)))
    return jnp.moveaxis(o, 0, 1).reshape(bsz, seqlen, ATTN_N_HEADS, ATTN_V_DIM)


def diff_attn_sample(q, k_past, v_past, k_new, v_new, lam):
    t_new = q.shape[1]
    past = k_past.shape[1]
    scale = ATTN_HEAD_DIM ** -0.5
    s_past = jnp.einsum('bqhjd,bkhjd->bhjqk', q, k_past).astype(jnp.float32) * scale
    s_new = jnp.einsum('bqhjd,bkhjd->bhjqk', q, k_new).astype(jnp.float32) * scale
    s_new = jnp.where(jnp.tril(jnp.ones((t_new, t_new), dtype=bool)), s_new, -jnp.inf)
    p = jax.nn.softmax(jnp.concatenate([s_past, s_new], axis=-1), axis=-1)
    att = (p[:, :, 0] - lam * p[:, :, 1]).astype(v_new.dtype)
    return (jnp.einsum('bhqk,bkhe->bqhe', att[..., :past], v_past) + jnp.einsum('bhqk,bkhe->bqhe', att[..., past:], v_new))


def diff_out(o, lam_init, subln, w_o):
    bsz, length = o.shape[:2]
    o = rmsnorm(o, subln) * (1.0 - lam_init)
    return o.reshape(bsz, length, ATTN_VO_DIM) @ w_o


def setup_inputs(seed: int = 0) -> dict:
    key = jax.random.key(seed)
    ks = jax.random.split(key, 40)
    f32 = jnp.float32
    n_pages = PAST_LEN // PAGE_SIZE
    n_used = DEC_BATCH * n_pages
    n_phys = n_used + max(1, n_used // 4)

    def nrm(k, shape, scale):
        return jax.random.normal(k, shape, f32) * scale

    def gain(k, shape):
        return 1.0 + 0.02 * jax.random.normal(k, shape, f32)

    dt0 = jnp.exp(jax.random.uniform(ks[10], (N_SSD_LAYERS, SSD_N_HEADS), f32) * (math.log(0.1) - math.log(0.001)) + math.log(0.001))
    dt_bias = dt0 + jnp.log(-jnp.expm1(-dt0))
    a_log = jnp.log(jax.random.uniform(ks[11], (N_SSD_LAYERS, SSD_N_HEADS), f32, 1.0, 16.0))
    page_table = jax.random.permutation(ks[6], n_phys)[:n_used].reshape(DEC_BATCH, n_pages).astype(jnp.int32)
    return {
        'x_prompt': nrm(ks[0], (BATCH, SEQ, D_MODEL), 1.0),
        'x_sample': nrm(ks[1], (DEC_BATCH, DEC_SEQ, D_MODEL), 1.0),
        'state_ssm': nrm(ks[2], (N_SSD_LAYERS, DEC_BATCH, SSD_N_HEADS, SSD_HEAD_DIM, SSD_D_STATE), 0.1),
        'state_conv': nrm(ks[3], (N_SSD_LAYERS, DEC_BATCH, SSD_D_CONV - 1, SSD_CONV_DIM), 1.0),
        'cache_k': nrm(ks[4], (n_phys, PAGE_SIZE, ATTN_N_HEADS, 2 * ATTN_HEAD_DIM), 1.0),
        'cache_v': nrm(ks[5], (n_phys, PAGE_SIZE, ATTN_N_HEADS, ATTN_V_DIM), 1.0),
        'page_table': page_table,
        'ssd_norm_pre': gain(ks[7], (N_SSD_LAYERS, D_MODEL)),
        'ssd_norm_post': gain(ks[8], (N_SSD_LAYERS, D_MODEL)),
        'ssd_w_in': nrm(ks[9], (N_SSD_LAYERS, D_MODEL, SSD_IN_DIM), D_MODEL ** -0.5),
        'ssd_conv_w': nrm(ks[12], (N_SSD_LAYERS, SSD_D_CONV, SSD_CONV_DIM), SSD_D_CONV ** -0.5),
        'ssd_conv_b': nrm(ks[13], (N_SSD_LAYERS, SSD_CONV_DIM), 0.02),
        'ssd_dt_bias': dt_bias,
        'ssd_a_log': a_log,
        'ssd_d': gain(ks[14], (N_SSD_LAYERS, SSD_N_HEADS)),
        'ssd_gate_norm': gain(ks[15], (N_SSD_LAYERS, SSD_D_INNER)),
        'ssd_w_out': nrm(ks[16], (N_SSD_LAYERS, SSD_D_INNER, D_MODEL), SSD_D_INNER ** -0.5),
        'mlp_norm_pre': gain(ks[17], (DEPTH, D_MODEL)),
        'mlp_norm_post': gain(ks[18], (DEPTH, D_MODEL)),
        'mlp_w_up': nrm(ks[19], (DEPTH, D_MODEL, D_FF), D_MODEL ** -0.5),
        'mlp_w_down': nrm(ks[20], (DEPTH, D_FF, D_MODEL), D_FF ** -0.5),
        'kv_norm': gain(ks[21], (D_MODEL,)),
        'w_kv': nrm(ks[22], (D_MODEL, ATTN_QK_DIM + ATTN_VO_DIM), D_MODEL ** -0.5),
        'attn_norm_pre': gain(ks[23], (N_ATTN_LAYERS, D_MODEL)),
        'attn_norm_post': gain(ks[24], (N_ATTN_LAYERS, D_MODEL)),
        'attn_w_q': nrm(ks[25], (N_ATTN_LAYERS, D_MODEL, ATTN_QK_DIM), D_MODEL ** -0.5),
        'attn_lambda_q1': nrm(ks[26], (N_ATTN_LAYERS, ATTN_HEAD_DIM), 0.1),
        'attn_lambda_k1': nrm(ks[27], (N_ATTN_LAYERS, ATTN_HEAD_DIM), 0.1),
        'attn_lambda_q2': nrm(ks[28], (N_ATTN_LAYERS, ATTN_HEAD_DIM), 0.1),
        'attn_lambda_k2': nrm(ks[29], (N_ATTN_LAYERS, ATTN_HEAD_DIM), 0.1),
        'attn_subln': gain(ks[30], (N_ATTN_LAYERS, ATTN_V_DIM)),
        'attn_w_o': nrm(ks[31], (N_ATTN_LAYERS, ATTN_VO_DIM, D_MODEL), ATTN_VO_DIM ** -0.5),
    }


def reference(x_prompt, x_sample, state_ssm, state_conv, cache_k, cache_v, page_table,
              ssd_norm_pre, ssd_norm_post, ssd_w_in, ssd_conv_w, ssd_conv_b, ssd_dt_bias, ssd_a_log,
              ssd_d, ssd_gate_norm, ssd_w_out, mlp_norm_pre, mlp_norm_post, mlp_w_up, mlp_w_down,
              kv_norm, w_kv, attn_norm_pre, attn_norm_post, attn_w_q, attn_lambda_q1, attn_lambda_k1,
              attn_lambda_q2, attn_lambda_k2, attn_subln, attn_w_o):
    pos_p = jnp.arange(x_prompt.shape[1])
    pos_s = PAST_LEN + jnp.arange(x_sample.shape[1])
    hp, hs = x_prompt, x_sample
    ssm_p, conv_p, ssm_s, conv_s = [], [], [], []
    k_p = v_p = k_s = v_s = k_past = v_past = None
    for i in range(DEPTH):
        if i < N_SSD_LAYERS:
            j = i
            w = (ssd_w_in[j], ssd_conv_w[j], ssd_conv_b[j], ssd_dt_bias[j], ssd_a_log[j], ssd_d[j], ssd_gate_norm[j], ssd_w_out[j])
            yp, sp, cp = ssd_mixer_prompt(rmsnorm(hp, ssd_norm_pre[j]), *w)
            ys, ss, cs = ssd_mixer_sample(rmsnorm(hs, ssd_norm_pre[j]), state_ssm[j], state_conv[j], *w)
            hp = hp + rmsnorm(yp, ssd_norm_post[j])
            hs = hs + rmsnorm(ys, ssd_norm_post[j])
            ssm_p.append(sp)
            conv_p.append(cp)
            ssm_s.append(ss)
            conv_s.append(cs)
        else:
            j = i - N_SSD_LAYERS
            if j == 0:
                k_p, v_p = shared_kv(hp, kv_norm, w_kv, pos_p)
                k_s, v_s = shared_kv(hs, kv_norm, w_kv, pos_s)
                n_seq = page_table.shape[0]
                k_past = cache_k[page_table].reshape(n_seq, -1, ATTN_N_HEADS, 2, ATTN_HEAD_DIM)
                v_past = cache_v[page_table].reshape(n_seq, -1, ATTN_N_HEADS, ATTN_V_DIM)
            lam_init = 0.8 - 0.6 * math.exp(-0.3 * i)
            lam = diff_lambda(attn_lambda_q1[j], attn_lambda_k1[j], attn_lambda_q2[j], attn_lambda_k2[j], lam_init)
            qp = diff_query(rmsnorm(hp, attn_norm_pre[j]), attn_w_q[j], pos_p)
            op = diff_attn_prompt(qp, k_p, v_p, lam)
            hp = hp + rmsnorm(diff_out(op, lam_init, attn_subln[j], attn_w_o[j]), attn_norm_post[j])
            qs = diff_query(rmsnorm(hs, attn_norm_pre[j]), attn_w_q[j], pos_s)
            os_ = diff_attn_sample(qs, k_past, v_past, k_s, v_s, lam)
            hs = hs + rmsnorm(diff_out(os_, lam_init, attn_subln[j], attn_w_o[j]), attn_norm_post[j])
        hp = hp + rmsnorm(squared_relu_mlp(rmsnorm(hp, mlp_norm_pre[i]), mlp_w_up[i], mlp_w_down[i]), mlp_norm_post[i])
        hs = hs + rmsnorm(squared_relu_mlp(rmsnorm(hs, mlp_norm_pre[i]), mlp_w_up[i], mlp_w_down[i]), mlp_norm_post[i])
    p_ssm = jnp.stack(ssm_p)
    p_conv = jnp.stack(conv_p)
    s_ssm = jnp.stack(ssm_s)
    s_conv = jnp.stack(conv_s)
    p_k = k_p.reshape(k_p.shape[0], k_p.shape[1], ATTN_N_HEADS, 2 * ATTN_HEAD_DIM)
    s_k = k_s.reshape(k_s.shape[0], k_s.shape[1], ATTN_N_HEADS, 2 * ATTN_HEAD_DIM)
    return (hp, hs, p_ssm, p_conv, p_k, v_p, s_ssm, s_conv, s_k, v_s)
```

```python
import functools
import math

import jax
import jax.numpy as jnp
from jax import lax
from jax.experimental import pallas as pl
from jax.experimental.pallas import tpu as pltpu

F32 = jnp.float32
BF16 = jnp.bfloat16

RMS_EPS = 1e-6
ROPE_THETA = 500000.0

V7X_LANES = 128
V7X_SUBLANES = 8
V7X_VMEM_BYTES = 64 * 1024 * 1024

SSD_HEAD_DIM = 64
SSD_N_GROUPS = 4
SSD_D_STATE = 128
SSD_D_CONV = 4
SSD_CHUNK = 128
ATTN_HEAD_DIM = 64
ROT_DIM = ATTN_HEAD_DIM // 4

NEG_BIG = -1e30


def _vmem_limit(nbytes):
    return int(min(V7X_VMEM_BYTES * 7 // 8, max(32 * 1024 * 1024, nbytes * 3 // 2)))


def _params(sem, vmem_bytes):
    return pltpu.CompilerParams(dimension_semantics=sem, vmem_limit_bytes=_vmem_limit(vmem_bytes))


def _const_spec(shape):
    nd = len(shape)
    return pl.BlockSpec(shape, lambda *_: (0,) * nd)


def _rms(x, g):
    ms = jnp.mean(x * x, axis=-1, keepdims=True)
    return x * lax.rsqrt(ms + RMS_EPS) * g


def _silu(x):
    return x * (1.0 / (1.0 + jnp.exp(-x)))


def _softplus(x):
    return jnp.maximum(x, 0.0) + jnp.log1p(jnp.exp(-jnp.abs(x)))


def _dot(a, b):
    return jnp.dot(a, b, preferred_element_type=F32)


def _dot_nt(a, b):
    return lax.dot_general(a, b, (((1,), (1,)), ((), ())), preferred_element_type=F32)


def _rope(t, cos, sa, sb):
    up = pltpu.roll(t, V7X_LANES - ROT_DIM // 2, 1)
    dn = pltpu.roll(t, ROT_DIM // 2, 1)
    return t * cos + up * sa + dn * sb


def _inproj_kernel(x_ref, g_ref, w_ref, z_ref, xbc_ref, dt_ref, *, d_inner, conv_dim):
    xn = _rms(x_ref[...], g_ref[...]).astype(BF16)
    z_ref[...] = _dot(xn, w_ref[:, 0:d_inner])
    xbc_ref[...] = _dot(xn, w_ref[:, d_inner:d_inner + conv_dim])
    dt_ref[...] = _dot(xn, w_ref[:, d_inner + conv_dim:])


def ssd_in_proj(x, g, w, *, d_inner, conv_dim, tm):
    m, d = x.shape
    n = w.shape[1]
    dtw = n - d_inner - conv_dim
    vm = 2 * tm * d * 4 + 2 * d * n * 2 + 2 * tm * n * 4
    return pl.pallas_call(
        functools.partial(_inproj_kernel, d_inner=d_inner, conv_dim=conv_dim),
        grid=(m // tm,),
        in_specs=[pl.BlockSpec((tm, d), lambda i: (i, 0)), _const_spec((1, d)), _const_spec((d, n))],
        out_specs=[pl.BlockSpec((tm, d_inner), lambda i: (i, 0)),
                   pl.BlockSpec((tm, conv_dim), lambda i: (i, 0)),
                   pl.BlockSpec((tm, dtw), lambda i: (i, 0))],
        out_shape=[jax.ShapeDtypeStruct((m, d_inner), F32),
                   jax.ShapeDtypeStruct((m, conv_dim), F32),
                   jax.ShapeDtypeStruct((m, dtw), F32)],
        compiler_params=_params(("parallel",), vm),
        name="ssd_in_proj",
    )(x, g, w)


def _qproj_kernel(x_ref, g_ref, w_ref, cos_ref, sa_ref, sb_ref, q_ref, *, n_heads, scale):
    xn = _rms(x_ref[...], g_ref[...]).astype(BF16)
    cos, sa, sb = cos_ref[...], sa_ref[...], sb_ref[...]
    for h in range(n_heads):
        sl = slice(h * V7X_LANES, (h + 1) * V7X_LANES)
        t = _dot(xn, w_ref[:, sl])
        q_ref[:, sl] = (_rope(t, cos, sa, sb) * scale).astype(BF16)


def attn_q_proj(x, g, w, tables, *, tm, n_pos_blocks):
    m, d = x.shape
    n = w.shape[1]
    n_heads = n // V7X_LANES
    tab_spec = pl.BlockSpec((tm, V7X_LANES), lambda i: (i % n_pos_blocks, 0))
    vm = 2 * tm * d * 4 + 2 * d * n * 2 + 2 * tm * n * 2 + 6 * tm * V7X_LANES * 4
    return pl.pallas_call(
        functools.partial(_qproj_kernel, n_heads=n_heads, scale=ATTN_HEAD_DIM ** -0.5),
        grid=(m // tm,),
        in_specs=[pl.BlockSpec((tm, d), lambda i: (i, 0)), _const_spec((1, d)), _const_spec((d, n)),
                  tab_spec, tab_spec, tab_spec],
        out_specs=pl.BlockSpec((tm, n), lambda i: (i, 0)),
        out_shape=jax.ShapeDtypeStruct((m, n), BF16),
        compiler_params=_params(("parallel",), vm),
        name="attn_q_proj",
    )(x, g, w, *tables)


def _kvproj_kernel(x_ref, g_ref, w_ref, cos_ref, sa_ref, sb_ref, k_ref, v_ref, kb_ref, vb_ref, *, n_heads):
    xn = _rms(x_ref[...], g_ref[...]).astype(BF16)
    cos, sa, sb = cos_ref[...], sa_ref[...], sb_ref[...]
    qk = n_heads * V7X_LANES
    for h in range(n_heads):
        sl = slice(h * V7X_LANES, (h + 1) * V7X_LANES)
        k = _rope(_dot(xn, w_ref[:, sl]), cos, sa, sb)
        k_ref[:, sl] = k
        kb_ref[:, sl] = k.astype(BF16)
    v = _dot(xn, w_ref[:, qk:])
    v_ref[...] = v
    vb_ref[...] = v.astype(BF16)


def shared_kv_proj(x, g, w, tables, *, tm, n_pos_blocks):
    m, d = x.shape
    n = w.shape[1]
    qk = n // 2
    n_heads = qk // V7X_LANES
    tab_spec = pl.BlockSpec((tm, V7X_LANES), lambda i: (i % n_pos_blocks, 0))
    row = lambda i: (i, 0)
    vm = 2 * tm * d * 4 + 2 * d * n * 2 + 2 * tm * n * 6 + 6 * tm * V7X_LANES * 4
    return pl.pallas_call(
        functools.partial(_kvproj_kernel, n_heads=n_heads),
        grid=(m // tm,),
        in_specs=[pl.BlockSpec((tm, d), row), _const_spec((1, d)), _const_spec((d, n)),
                  tab_spec, tab_spec, tab_spec],
        out_specs=[pl.BlockSpec((tm, qk), row)] * 4,
        out_shape=[jax.ShapeDtypeStruct((m, qk), F32), jax.ShapeDtypeStruct((m, qk), F32),
                   jax.ShapeDtypeStruct((m, qk), BF16), jax.ShapeDtypeStruct((m, qk), BF16)],
        compiler_params=_params(("parallel",), vm),
        name="shared_kv_proj",
    )(x, g, w, *tables)


def _outproj_kernel(a_ref, w_ref, g_ref, h_ref, o_ref):
    y = _dot(a_ref[...].astype(BF16), w_ref[...])
    o_ref[...] = h_ref[...] + _rms(y, g_ref[...])


def out_proj_residual(a, w, g_post, h, *, tm):
    m, k = a.shape
    d = w.shape[1]
    row = lambda i: (i, 0)
    vm = 2 * tm * k * a.dtype.itemsize + 2 * k * d * 2 + 4 * tm * d * 4
    return pl.pallas_call(
        _outproj_kernel,
        grid=(m // tm,),
        in_specs=[pl.BlockSpec((tm, k), row), _const_spec((k, d)), _const_spec((1, d)),
                  pl.BlockSpec((tm, d), row)],
        out_specs=pl.BlockSpec((tm, d), row),
        out_shape=jax.ShapeDtypeStruct((m, d), F32),
        compiler_params=_params(("parallel",), vm),
        name="out_proj_residual",
    )(a, w, g_post, h)


def _mlp_kernel(h_ref, g1_ref, wu_ref, wd_ref, g2_ref, o_ref, *, tf):
    h = h_ref[...]
    xn = _rms(h, g1_ref[...]).astype(BF16)
    d_ff = wu_ref.shape[1]
    acc = jnp.zeros(h.shape, F32)
    for f in range(d_ff // tf):
        u = jnp.maximum(_dot(xn, wu_ref[:, f * tf:(f + 1) * tf]), 0.0)
        acc = acc + _dot((u * u).astype(BF16), wd_ref[f * tf:(f + 1) * tf, :])
    o_ref[...] = h + _rms(acc, g2_ref[...])


def mlp_residual(h, g_pre, w_up, w_down, g_post, *, tm, tf=512):
    m, d = h.shape
    d_ff = w_up.shape[1]
    row = lambda i: (i, 0)
    vm = 4 * tm * d * 4 + 2 * 2 * d * d_ff * 2 + tm * d * 4 + 2 * tm * tf * 4
    return pl.pallas_call(
        functools.partial(_mlp_kernel, tf=tf),
        grid=(m // tm,),
        in_specs=[pl.BlockSpec((tm, d), row), _const_spec((1, d)), _const_spec((d, d_ff)),
                  _const_spec((d_ff, d)), _const_spec((1, d))],
        out_specs=pl.BlockSpec((tm, d), row),
        out_shape=jax.ShapeDtypeStruct((m, d), F32),
        compiler_params=_params(("parallel",), vm),
        name="mlp_residual",
    )(h, g_pre, w_up, w_down, g_post)


def _expand3(v, e):
    hi = v.astype(BF16)
    r1 = v - hi.astype(F32)
    mid = r1.astype(BF16)
    lo = (r1 - mid.astype(F32)).astype(BF16)
    return _dot(hi, e) + _dot(mid, e) + _dot(lo, e)


def _gated_group_norm(y, z, gate_norm, n_groups):
    g = y * _silu(z)
    width = g.shape[1] // n_groups
    outs = []
    for i in range(n_groups):
        gg = g[:, i * width:(i + 1) * width]
        ms = jnp.mean(gg * gg, axis=-1, keepdims=True)
        outs.append(gg * lax.rsqrt(ms + RMS_EPS) * gate_norm[:, i * width:(i + 1) * width])
    return outs


def _ssd_prompt_kernel(z_ref, xbc_ref, dtr_ref, cw_ref, cb_ref, dtb_ref, alog_ref, dexp_ref, gn_ref,
                       tri_ref, e128_ref, e64_ref,
                       g_ref, ssm_ref, conv_ref,
                       cbuf, st_t, y_s, ab_s, *, n_heads, d_inner):
    L = SSD_CHUNK
    N = SSD_D_STATE
    gn_w = SSD_N_GROUPS * N
    heads_per_group = n_heads // SSD_N_GROUPS
    c = pl.program_id(1)
    last = pl.num_programs(1) - 1
    halo = V7X_SUBLANES

    @pl.when(c == 0)
    def _():
        cbuf[0:halo, :] = jnp.zeros((halo, cbuf.shape[1]), F32)
        st_t[...] = jnp.zeros(st_t.shape, F32)

    cbuf[halo:halo + L, :] = xbc_ref[...]
    acc = cb_ref[...]
    for k in range(SSD_D_CONV):
        off = halo - (SSD_D_CONV - 1) + k
        acc = acc + cbuf[off:off + L, :] * cw_ref[k:k + 1, :]
    act = _silu(acc)
    x = act[:, :d_inner]

    @pl.when(c == last)
    def _():
        conv_ref[...] = cbuf[L:L + halo, :]

    cbuf[0:halo, :] = cbuf[L:L + halo, :]

    dt = _softplus(dtr_ref[...] + dtb_ref[...])
    a = -jnp.exp(alog_ref[...])
    acum = jnp.dot(tri_ref[...], a * dt, precision=lax.Precision.HIGHEST, preferred_element_type=F32)
    acum_last = acum[L - 1:L, :]
    ab_s[...] = _expand3(acum, e128_ref[...])
    acum_t = acum.T
    dt_t = dt.T
    w_t = (jnp.exp(acum_last - acum) * dt).T
    cd = _expand3(jnp.broadcast_to(jnp.exp(acum_last), (V7X_SUBLANES, V7X_LANES)), e64_ref[...])[0:1, :]

    li = lax.broadcasted_iota(jnp.int32, (L, L), 0)
    si = lax.broadcasted_iota(jnp.int32, (L, L), 1)
    causal = li >= si
    lo_half = lax.broadcasted_iota(jnp.int32, (L, V7X_LANES), 1) < SSD_HEAD_DIM

    for grp in range(SSD_N_GROUPS):
        b_g = act[:, d_inner + grp * N:d_inner + (grp + 1) * N]
        c_g = act[:, d_inner + gn_w + grp * N:d_inner + gn_w + (grp + 1) * N]
        cb_g = _dot_nt(c_g.astype(BF16), b_g.astype(BF16))
        bt_g = b_g.T
        for pr in range(heads_per_group // 2):
            q = grp * (heads_per_group // 2) + pr
            sl = slice(q * V7X_LANES, (q + 1) * V7X_LANES)
            xp = x[:, sl].astype(BF16)
            st_prev = st_t[:, sl]
            rhs = jnp.concatenate([xp, st_prev.astype(BF16)], axis=0)
            ys, ss = [], []
            for h in (2 * q, 2 * q + 1):
                col = ab_s[:, h * V7X_LANES:(h + 1) * V7X_LANES]
                seg = col - acum_t[h:h + 1, :]
                wm = jnp.exp(jnp.where(causal, seg, NEG_BIG)) * (cb_g * dt_t[h:h + 1, :])
                ce = c_g * jnp.exp(col)
                lhs = jnp.concatenate([wm.astype(BF16), ce.astype(BF16)], axis=1)
                ys.append(_dot(lhs, rhs))
                ss.append(_dot((bt_g * w_t[h:h + 1, :]).astype(BF16), xp))
            y_s[:, sl] = jnp.where(lo_half, ys[0], ys[1])
            st_t[:, sl] = st_prev * cd[:, sl] + jnp.where(lo_half, ss[0], ss[1])

    y = y_s[...] + x * dexp_ref[...]
    outs = _gated_group_norm(y, z_ref[...], gn_ref[...], SSD_N_GROUPS)
    width = d_inner // SSD_N_GROUPS
    for i, o in enumerate(outs):
        g_ref[:, i * width:(i + 1) * width] = o.astype(BF16)

    @pl.when(c == last)
    def _():
        for q in range(n_heads // 2):
            sl = slice(q * V7X_LANES, (q + 1) * V7X_LANES)
            ssm_ref[sl, :] = st_t[:, sl].T


def ssd_prompt(z, xbc, dtr, cw, cb, dtb, alog, dexp, gn, consts, *, bsz, seqlen, n_heads):
    m, d_inner = z.shape
    conv_dim = xbc.shape[1]
    L = SSD_CHUNK
    nc = seqlen // L
    tri, e128, e64 = consts
    blk = lambda b, c: (b * nc + c, 0)
    vm = (2 * L * (d_inner + conv_dim + V7X_LANES) * 4 + 2 * L * d_inner * 2
          + (L + 8) * conv_dim * 4 + 2 * L * d_inner * 4 + L * n_heads * V7X_LANES * 4
          + 2 * (e128.size + e64.size) * 2 + 4 * n_heads * SSD_HEAD_DIM * SSD_D_STATE * 4
          + 8 * L * conv_dim * 4)
    return pl.pallas_call(
        functools.partial(_ssd_prompt_kernel, n_heads=n_heads, d_inner=d_inner),
        grid=(bsz, nc),
        in_specs=[pl.BlockSpec((L, d_inner), blk), pl.BlockSpec((L, conv_dim), blk),
                  pl.BlockSpec((L, V7X_LANES), blk),
                  _const_spec(cw.shape), _const_spec(cb.shape), _const_spec(dtb.shape),
                  _const_spec(alog.shape), _const_spec(dexp.shape), _const_spec(gn.shape),
                  _const_spec(tri.shape), _const_spec(e128.shape), _const_spec(e64.shape)],
        out_specs=[pl.BlockSpec((L, d_inner), blk),
                   pl.BlockSpec((None, n_heads * SSD_HEAD_DIM, SSD_D_STATE), lambda b, c: (b, 0, 0)),
                   pl.BlockSpec((None, V7X_SUBLANES, conv_dim), lambda b, c: (b, 0, 0))],
        out_shape=[jax.ShapeDtypeStruct((m, d_inner), BF16),
                   jax.ShapeDtypeStruct((bsz, n_heads * SSD_HEAD_DIM, SSD_D_STATE), F32),
                   jax.ShapeDtypeStruct((bsz, V7X_SUBLANES, conv_dim), F32)],
        scratch_shapes=[pltpu.VMEM((L + V7X_SUBLANES, conv_dim), F32),
                        pltpu.VMEM((SSD_D_STATE, d_inner), F32),
                        pltpu.VMEM((L, d_inner), F32),
                        pltpu.VMEM((L, n_heads * V7X_LANES), F32)],
        compiler_params=_params(("arbitrary", "arbitrary"), vm),
        name="ssd_prompt",
    )(z, xbc, dtr, cw, cb, dtb, alog, dexp, gn, tri, e128, e64)


def _ssd_sample_conv_kernel(xbc_ref, cs_ref, dtr_ref, cw_ref, cb_ref, dtb_ref,
                            x_ref, b_ref, c_ref, dt_ref, ncs_ref, *, d_inner):
    new = xbc_ref[...]
    acc = cb_ref[...] + new * cw_ref[SSD_D_CONV - 1:SSD_D_CONV, :]
    for k in range(SSD_D_CONV - 1):
        acc = acc + cs_ref[k] * cw_ref[k:k + 1, :]
    act = _silu(acc)
    gn_w = SSD_N_GROUPS * SSD_D_STATE
    x_ref[...] = act[:, :d_inner]
    b_ref[...] = act[:, d_inner:d_inner + gn_w]
    c_ref[...] = act[:, d_inner + gn_w:]
    dt_ref[...] = _softplus(dtr_ref[...] + dtb_ref[...])
    for k in range(SSD_D_CONV - 2):
        ncs_ref[k] = cs_ref[k + 1]
    ncs_ref[SSD_D_CONV - 2] = new


def ssd_sample_conv(xbc, cs_t, dtr, cw, cb, dtb, *, d_inner):
    n, conv_dim = xbc.shape
    gn_w = SSD_N_GROUPS * SSD_D_STATE
    args = (xbc, cs_t, dtr, cw, cb, dtb)
    vm = 6 * xbc.size * 4 + 4 * cs_t.size * 4
    return pl.pallas_call(
        functools.partial(_ssd_sample_conv_kernel, d_inner=d_inner),
        grid=(1,),
        in_specs=[_const_spec(a.shape) for a in args],
        out_specs=[_const_spec((n, d_inner)), _const_spec((n, gn_w)), _const_spec((n, gn_w)),
                   _const_spec(dtr.shape), _const_spec(cs_t.shape)],
        out_shape=[jax.ShapeDtypeStruct((n, d_inner), F32), jax.ShapeDtypeStruct((n, gn_w), F32),
                   jax.ShapeDtypeStruct((n, gn_w), F32), jax.ShapeDtypeStruct(dtr.shape, F32),
                   jax.ShapeDtypeStruct(cs_t.shape, F32)],
        compiler_params=_params(("arbitrary",), vm),
        name="ssd_sample_conv",
    )(*args)


def _ssd_sample_state_kernel(st_ref, xt_ref, dtb_ref, dtr_ref, alog_ref, b_ref, c_ref,
                             nst_ref, y_ref, *, sb, n_heads):
    heads_per_group = n_heads // SSD_N_GROUPS
    N = SSD_D_STATE
    P = SSD_HEAD_DIM
    a_rep = -jnp.exp(alog_ref[...])
    for s in range(sb):
        dec = jnp.exp(a_rep * dtb_ref[s])
        xdt_t = xt_ref[s] * dtr_ref[s]
        for grp in range(SSD_N_GROUPS):
            b_row = b_ref[s][:, grp * N:(grp + 1) * N]
            c_row = c_ref[s][:, grp * N:(grp + 1) * N]
            news = []
            for r in range(heads_per_group):
                h = grp * heads_per_group + r
                new = st_ref[s, h] * dec[h:h + 1, :] + xdt_t[:, h:h + 1] * b_row
                nst_ref[s, h] = new
                news.append(new)
            hg = jnp.concatenate(news, axis=0).astype(BF16)
            c8 = jnp.broadcast_to(c_row, (V7X_SUBLANES, N)).astype(BF16)
            yg = _dot_nt(c8, hg)
            y_ref[s, :, grp * heads_per_group * P:(grp + 1) * heads_per_group * P] = yg[0:1, :]


def ssd_sample_state(state, x_t, dt_bcast, dt_row, alog_rep, b_in, c_in, *, sb):
    n, n_heads, P, N = state.shape
    d_inner = n_heads * P
    gn_w = b_in.shape[-1]
    blk4 = lambda i: (i, 0, 0, 0)
    blk3 = lambda i: (i, 0, 0)
    vm = 4 * sb * n_heads * P * N * 4 + 4 * sb * (P * V7X_LANES + n_heads * V7X_LANES) * 4
    return pl.pallas_call(
        functools.partial(_ssd_sample_state_kernel, sb=sb, n_heads=n_heads),
        grid=(n // sb,),
        in_specs=[pl.BlockSpec((sb, n_heads, P, N), blk4),
                  pl.BlockSpec((sb, P, n_heads), blk3),
                  pl.BlockSpec((sb, n_heads, V7X_LANES), blk3),
                  pl.BlockSpec((sb, 1, n_heads), blk3),
                  _const_spec(alog_rep.shape),
                  pl.BlockSpec((sb, 1, gn_w), blk3),
                  pl.BlockSpec((sb, 1, gn_w), blk3)],
        out_specs=[pl.BlockSpec((sb, n_heads, P, N), blk4),
                   pl.BlockSpec((sb, 1, d_inner), blk3)],
        out_shape=[jax.ShapeDtypeStruct(state.shape, F32),
                   jax.ShapeDtypeStruct((n, 1, d_inner), F32)],
        compiler_params=_params(("parallel",), vm),
        name="ssd_sample_state",
    )(state, x_t, dt_bcast, dt_row, alog_rep, b_in, c_in)


def _ssd_sample_gate_kernel(y_ref, x_ref, z_ref, dexp_ref, gn_ref, g_ref):
    y = y_ref[...] + x_ref[...] * dexp_ref[...]
    outs = _gated_group_norm(y, z_ref[...], gn_ref[...], SSD_N_GROUPS)
    width = y.shape[1] // SSD_N_GROUPS
    for i, o in enumerate(outs):
        g_ref[:, i * width:(i + 1) * width] = o.astype(BF16)


def ssd_sample_gate(y, x, z, dexp, gn):
    args = (y, x, z, dexp, gn)
    return pl.pallas_call(
        _ssd_sample_gate_kernel,
        grid=(1,),
        in_specs=[_const_spec(a.shape) for a in args],
        out_specs=_const_spec(y.shape),
        out_shape=jax.ShapeDtypeStruct(y.shape, BF16),
        compiler_params=_params(("arbitrary",), 8 * y.size * 4),
        name="ssd_sample_gate",
    )(*args)


def _diff_lambda(lam_ref, lam_init):
    v = lam_ref[...]
    d1 = jnp.sum(v[0:1, :] * v[1:2, :], axis=-1, keepdims=True)
    d2 = jnp.sum(v[2:3, :] * v[3:4, :], axis=-1, keepdims=True)
    return jnp.exp(d1) - jnp.exp(d2) + lam_init


def _flash_kernel(lam_ref, q_ref, k_ref, vt_ref, subln_ref, o_ref, m_s, l_s, acc_s, *, t, lam_init):
    qi = pl.program_id(2)
    q = q_ref[...]
    lane = lax.broadcasted_iota(jnp.int32, q.shape, 1)
    zero = jnp.zeros_like(q)
    q2 = jnp.concatenate([jnp.where(lane < ATTN_HEAD_DIM, q, zero),
                          jnp.where(lane >= ATTN_HEAD_DIM, q, zero)], axis=0)
    m_s[...] = jnp.full(m_s.shape, NEG_BIG, F32)
    l_s[...] = jnp.zeros(l_s.shape, F32)
    acc_s[...] = jnp.zeros(acc_s.shape, F32)

    def step(j, masked):
        start = pl.multiple_of(j * t, t)
        kb = k_ref[pl.ds(start, t), :]
        vtb = vt_ref[:, pl.ds(start, t)]
        s = _dot_nt(kb, q2)
        if masked:
            kv = lax.broadcasted_iota(jnp.int32, s.shape, 0)
            qq = lax.broadcasted_iota(jnp.int32, s.shape, 1)
            qq = jnp.where(qq >= t, qq - t, qq)
            s = jnp.where(kv <= qq, s, NEG_BIG)
        m_old = m_s[...]
        m_new = jnp.maximum(m_old, jnp.max(s, axis=0, keepdims=True))
        alpha = jnp.exp(m_old - m_new)
        p = jnp.exp(s - m_new)
        l_s[...] = alpha * l_s[...] + jnp.sum(p, axis=0, keepdims=True)
        acc_s[...] = alpha * acc_s[...] + _dot(vtb, p.astype(BF16))
        m_s[...] = m_new

    def body(j, carry):
        step(j, False)
        return carry

    lax.fori_loop(0, qi, body, 0)
    step(qi, True)

    lam = _diff_lambda(lam_ref, lam_init)
    inv = 1.0 / l_s[...]
    acc = acc_s[...] * inv
    o_t = acc[:, :t] - lam * acc[:, t:]
    ms = jnp.mean(o_t * o_t, axis=0, keepdims=True)
    o_t = o_t * lax.rsqrt(ms + RMS_EPS)
    o = o_t.T * (subln_ref[...] * (1.0 - lam_init))
    o_ref[...] = o.astype(BF16)


def diff_attn_prompt(lam_vecs, q, k, v_t, subln, *, bsz, seqlen, n_heads, lam_init, t=256):
    nq = seqlen // t
    vm = 2 * (t * 128 * 2 + 2 * seqlen * 128 * 2 + t * 128 * 2) + 4 * t * 128 * 4 + 6 * t * 2 * t * 4
    return pl.pallas_call(
        functools.partial(_flash_kernel, t=t, lam_init=lam_init),
        grid=(bsz, n_heads, nq),
        in_specs=[_const_spec(lam_vecs.shape),
                  pl.BlockSpec((t, V7X_LANES), lambda b, h, i: (b * nq + i, h)),
                  pl.BlockSpec((seqlen, V7X_LANES), lambda b, h, i: (b, h)),
                  pl.BlockSpec((None, None, V7X_LANES, seqlen), lambda b, h, i: (b, h, 0, 0)),
                  _const_spec(subln.shape)],
        out_specs=pl.BlockSpec((t, V7X_LANES), lambda b, h, i: (b * nq + i, h)),
        out_shape=jax.ShapeDtypeStruct(q.shape, BF16),
        scratch_shapes=[pltpu.VMEM((1, 2 * t), F32), pltpu.VMEM((1, 2 * t), F32),
                        pltpu.VMEM((V7X_LANES, 2 * t), F32)],
        compiler_params=_params(("parallel", "parallel", "arbitrary"), vm),
        name="diff_attn_prompt",
    )(lam_vecs, q, k, v_t, subln)


def _decode_kernel(pt_ref, lam_ref, q_ref, kn_ref, vn_ref, subln_ref, *rest, n_pages, n_heads, lam_init):
    k_refs = rest[:n_pages]
    v_refs = rest[n_pages:2 * n_pages]
    o_ref = rest[2 * n_pages]
    del pt_ref
    width = n_heads * V7X_LANES
    rows = 2 * n_heads
    q = q_ref[0]
    r_i = lax.broadcasted_iota(jnp.int32, (rows, width), 0)
    l_i = lax.broadcasted_iota(jnp.int32, (rows, width), 1)
    head_of_row = jnp.where(r_i >= n_heads, r_i - n_heads, r_i)
    map_of_row = jnp.where(r_i >= n_heads, 1, 0)
    head_of_lane = l_i // V7X_LANES
    map_of_lane = (l_i // ATTN_HEAD_DIM) % 2
    own_head = head_of_row == head_of_lane
    own_map = jnp.where(map_of_row == map_of_lane, 1.0, 0.0)
    wq_t = (jnp.where(own_head, own_map, 0.0) * q.astype(F32)).astype(BF16)

    s_parts = [_dot_nt(wq_t, k_refs[p][0].astype(BF16)) for p in range(n_pages)]
    s_past = jnp.concatenate(s_parts, axis=1)
    kn8 = jnp.broadcast_to(kn_ref[0], (V7X_SUBLANES, width)).astype(BF16)
    s_new = _dot_nt(wq_t, kn8)[:, 0:1]
    m = jnp.maximum(jnp.max(s_past, axis=1, keepdims=True), s_new)
    p_past = jnp.exp(s_past - m)
    p_new = jnp.exp(s_new - m)
    denom = jnp.sum(p_past, axis=1, keepdims=True) + p_new
    pb = p_past.astype(BF16)
    page = k_refs[0].shape[1]
    acc = p_new * vn_ref[0]
    for p in range(n_pages):
        acc = acc + _dot(pb[:, p * page:(p + 1) * page], v_refs[p][0].astype(BF16))

    lam = _diff_lambda(lam_ref, lam_init)
    acc = jnp.where(own_head, acc, 0.0) * (1.0 / denom)
    d = acc[:n_heads, :] - lam * acc[n_heads:, :]
    ms = jnp.sum(d * d, axis=1, keepdims=True) * (1.0 / V7X_LANES)
    d = d * lax.rsqrt(ms + RMS_EPS)
    o = jnp.sum(d, axis=0, keepdims=True) * (subln_ref[...] * (1.0 - lam_init))
    o_ref[0] = o


def diff_attn_decode(page_table, lam_vecs, q, k_new, v_new, subln_t, cache_k, cache_v, *, lam_init):
    n, n_pages = page_table.shape
    _, page, width = cache_k.shape
    n_heads = width // V7X_LANES
    row = lambda b, pt: (b, 0, 0)
    const2 = lambda b, pt: (0, 0)
    page_specs = [pl.BlockSpec((1, page, width), functools.partial(lambda b, pt, i: (pt[b, i], 0, 0), i=i))
                  for i in range(n_pages)]
    vm = 2 * 2 * n_pages * page * width * 4 + 2 * n_pages * page * width * 2
    grid_spec = pltpu.PrefetchScalarGridSpec(
        num_scalar_prefetch=1,
        grid=(n,),
        in_specs=[pl.BlockSpec(lam_vecs.shape, const2),
                  pl.BlockSpec((1, 1, width), row), pl.BlockSpec((1, 1, width), row),
                  pl.BlockSpec((1, 1, width), row), pl.BlockSpec(subln_t.shape, const2)]
                 + page_specs + page_specs,
        out_specs=pl.BlockSpec((1, 1, width), row),
    )
    return pl.pallas_call(
        functools.partial(_decode_kernel, n_pages=n_pages, n_heads=n_heads, lam_init=lam_init),
        grid_spec=grid_spec,
        out_shape=jax.ShapeDtypeStruct((n, 1, width), F32),
        compiler_params=_params(("parallel",), vm),
        name="diff_attn_decode",
    )(page_table, lam_vecs, q, k_new, v_new, subln_t, *([cache_k] * n_pages), *([cache_v] * n_pages))


def _rope_tables(pos):
    half = ROT_DIM // 2
    inv = ROPE_THETA ** (-jnp.arange(0, ROT_DIM, 2, dtype=F32) / ROT_DIM)
    ang = pos.astype(F32)[:, None] * inv[None, :]
    cos, sin = jnp.cos(ang), jnp.sin(ang)
    n = pos.shape[0]
    pad = jnp.zeros((n, ATTN_HEAD_DIM - ROT_DIM), F32)
    zeros = jnp.zeros((n, half), F32)
    c_map = jnp.concatenate([cos, cos, pad + 1.0], axis=1)
    sa_map = jnp.concatenate([-sin, zeros, pad], axis=1)
    sb_map = jnp.concatenate([zeros, sin, pad], axis=1)
    return tuple(jnp.concatenate([t, t], axis=1) for t in (c_map, sa_map, sb_map))


def _row(v, width=None):
    v = v.astype(F32).reshape(1, -1)
    if width is not None and v.shape[1] < width:
        v = jnp.pad(v, ((0, 0), (0, width - v.shape[1])))
    return v


def kernel(x_prompt, x_sample, state_ssm, state_conv, cache_k, cache_v, page_table, ssd_norm_pre, ssd_norm_post, ssd_w_in, ssd_conv_w, ssd_conv_b, ssd_dt_bias, ssd_a_log, ssd_d, ssd_gate_norm, ssd_w_out, mlp_norm_pre, mlp_norm_post, mlp_w_up, mlp_w_down, kv_norm, w_kv, attn_norm_pre, attn_norm_post, attn_w_q, attn_lambda_q1, attn_lambda_k1, attn_lambda_q2, attn_lambda_k2, attn_subln, attn_w_o):
    bsz, seqlen, d_model = x_prompt.shape
    n_dec = x_sample.shape[0]
    n_ssd = ssd_w_in.shape[0]
    depth = mlp_w_up.shape[0]
    n_heads_ssd = ssd_a_log.shape[1]
    d_inner = n_heads_ssd * SSD_HEAD_DIM
    conv_dim = ssd_conv_w.shape[2]
    qk_dim = attn_w_q.shape[2]
    n_heads = qk_dim // V7X_LANES
    m_p = bsz * seqlen

    hp = x_prompt.reshape(m_p, d_model)
    hs = x_sample.reshape(n_dec, d_model)

    tm_p = 256
    tm_mlp = 512
    tm_s = n_dec

    L = SSD_CHUNK
    tri = (jnp.arange(L)[:, None] >= jnp.arange(L)[None, :]).astype(F32)
    hrow = jnp.arange(V7X_LANES)[:, None]
    e128 = (hrow == (jnp.arange(n_heads_ssd * V7X_LANES)[None, :] // V7X_LANES)).astype(BF16)
    e64 = (hrow == (jnp.arange(d_inner)[None, :] // SSD_HEAD_DIM)).astype(BF16)

    tab_p = _rope_tables(jnp.arange(seqlen))
    past_len = page_table.shape[1] * cache_k.shape[1]
    tab_s = tuple(jnp.broadcast_to(t, (n_dec, V7X_LANES)) for t in _rope_tables(jnp.full((1,), past_len)))

    ssm_p, conv_p, ssm_s, conv_s = [], [], [], []
    k_p = v_p = k_s = v_s = None
    kb_p = vt_p = None
    ck = cache_k.reshape(cache_k.shape[0], cache_k.shape[1], -1)
    cv = cache_v.reshape(cache_v.shape[0], cache_v.shape[1], -1)

    for i in range(depth):
        if i < n_ssd:
            j = i
            w_in = ssd_w_in[j]
            w_in_b = jnp.concatenate(
                [w_in[:, :d_inner + conv_dim],
                 jnp.pad(w_in[:, d_inner + conv_dim:], ((0, 0), (0, V7X_LANES - n_heads_ssd)))], axis=1).astype(BF16)
            w_out_b = ssd_w_out[j].astype(BF16)
            g_pre = _row(ssd_norm_pre[j])
            g_post = _row(ssd_norm_post[j])
            cw = ssd_conv_w[j].astype(F32)
            cb = _row(ssd_conv_b[j])
            dtb = _row(ssd_dt_bias[j], V7X_LANES)
            alog = _row(ssd_a_log[j], V7X_LANES)
            dexp = jnp.repeat(ssd_d[j].astype(F32), SSD_HEAD_DIM).reshape(1, d_inner)
            gn = _row(ssd_gate_norm[j])

            z, xbc, dtr = ssd_in_proj(hp, g_pre, w_in_b, d_inner=d_inner, conv_dim=conv_dim, tm=tm_p)
            g, sp, cp = ssd_prompt(z, xbc, dtr, cw, cb, dtb, alog, dexp, gn, (tri, e128, e64),
                                   bsz=bsz, seqlen=seqlen, n_heads=n_heads_ssd)
            hp = out_proj_residual(g, w_out_b, g_post, hp, tm=tm_p)
            ssm_p.append(sp.reshape(bsz, n_heads_ssd, SSD_HEAD_DIM, SSD_D_STATE))
            conv_p.append(cp[:, V7X_SUBLANES - (SSD_D_CONV - 1):, :])

            zs, xbcs, dtrs = ssd_in_proj(hs, g_pre, w_in_b, d_inner=d_inner, conv_dim=conv_dim, tm=tm_s)
            cs_t = jnp.transpose(state_conv[j].astype(F32), (1, 0, 2))
            xs, bs_, cs_, dts, ncs = ssd_sample_conv(xbcs, cs_t, dtrs, cw, cb, dtb, d_inner=d_inner)
            x_t = jnp.transpose(xs.reshape(n_dec, n_heads_ssd, SSD_HEAD_DIM), (0, 2, 1))
            dt_h = dts[:, :n_heads_ssd]
            dt_bcast = jnp.broadcast_to(dt_h[:, :, None], (n_dec, n_heads_ssd, V7X_LANES))
            alog_rep = jnp.broadcast_to(ssd_a_log[j].astype(F32)[:, None], (n_heads_ssd, V7X_LANES))
            nst, ys = ssd_sample_state(state_ssm[j].astype(F32), x_t, dt_bcast, dt_h[:, None, :], alog_rep,
                                       bs_[:, None, :], cs_[:, None, :], sb=4)
            gs = ssd_sample_gate(ys.reshape(n_dec, d_inner), xs, zs, dexp, gn)
            hs = out_proj_residual(gs, w_out_b, g_post, hs, tm=tm_s)
            ssm_s.append(nst.astype(state_ssm.dtype))
            conv_s.append(jnp.transpose(ncs, (1, 0, 2)).astype(state_conv.dtype))
        else:
            j = i - n_ssd
            if j == 0:
                w_kv_b = w_kv.astype(BF16)
                g_kv = _row(kv_norm)
                k_p, v_p, kb_p, vb_p = shared_kv_proj(hp, g_kv, w_kv_b, tab_p, tm=tm_p, n_pos_blocks=seqlen // tm_p)
                k_s, v_s, _, _ = shared_kv_proj(hs, g_kv, w_kv_b, tab_s, tm=tm_s, n_pos_blocks=1)
                vt_p = jnp.transpose(vb_p.reshape(bsz, seqlen, n_heads, V7X_LANES), (0, 2, 3, 1))
            lam_init = 0.8 - 0.6 * math.exp(-0.3 * i)
            lam_vecs = jnp.pad(
                jnp.stack([attn_lambda_q1[j], attn_lambda_k1[j], attn_lambda_q2[j], attn_lambda_k2[j]]).astype(F32),
                ((0, V7X_SUBLANES - 4), (0, V7X_LANES - ATTN_HEAD_DIM)))
            g_pre = _row(attn_norm_pre[j])
            g_post = _row(attn_norm_post[j])
            w_q_b = attn_w_q[j].astype(BF16)
            w_o_b = attn_w_o[j].astype(BF16)
            subln = _row(attn_subln[j])
            subln_t = jnp.tile(subln, (1, n_heads))

            qp = attn_q_proj(hp, g_pre, w_q_b, tab_p, tm=tm_p, n_pos_blocks=seqlen // tm_p)
            op = diff_attn_prompt(lam_vecs, qp, kb_p, vt_p, subln, bsz=bsz, seqlen=seqlen, n_heads=n_heads,
                                  lam_init=lam_init)
            hp = out_proj_residual(op, w_o_b, g_post, hp, tm=tm_p)

            qs = attn_q_proj(hs, g_pre, w_q_b, tab_s, tm=tm_s, n_pos_blocks=1)
            os_ = diff_attn_decode(page_table, lam_vecs, qs[:, None, :], k_s[:, None, :], v_s[:, None, :],
                                   subln_t, ck, cv, lam_init=lam_init)
            hs = out_proj_residual(os_.reshape(n_dec, qk_dim), w_o_b, g_post, hs, tm=tm_s)

        g1 = _row(mlp_norm_pre[i])
        g2 = _row(mlp_norm_post[i])
        wu = mlp_w_up[i].astype(BF16)
        wd = mlp_w_down[i].astype(BF16)
        hp = mlp_residual(hp, g1, wu, wd, g2, tm=tm_mlp)
        hs = mlp_residual(hs, g1, wu, wd, g2, tm=tm_s)

    y_prompt = hp.reshape(bsz, seqlen, d_model)
    y_sample = hs.reshape(n_dec, 1, d_model)
    p_k = k_p.reshape(bsz, seqlen, n_heads, V7X_LANES)
    p_v = v_p.reshape(bsz, seqlen, n_heads, V7X_LANES)
    s_k = k_s.reshape(n_dec, 1, n_heads, V7X_LANES)
    s_v = v_s.reshape(n_dec, 1, n_heads, V7X_LANES)
    return (y_prompt, y_sample, jnp.stack(ssm_p), jnp.stack(conv_p), p_k, p_v,
            jnp.stack(ssm_s), jnp.stack(conv_s), s_k, s_v)
```

```python
import functools
import math

import jax
import jax.numpy as jnp
from jax import lax
from jax.experimental import pallas as pl
from jax.experimental.pallas import tpu as pltpu

F32 = jnp.float32
BF16 = jnp.bfloat16

RMS_EPS = 1e-6
ROPE_THETA = 500000.0

V7X_LANES = 128
V7X_SUBLANES = 8
V7X_VMEM_BYTES = 64 * 1024 * 1024

SSD_HEAD_DIM = 64
SSD_N_GROUPS = 4
SSD_D_STATE = 128
SSD_D_CONV = 4
SSD_CHUNK = 128
ATTN_HEAD_DIM = 64
ROT_DIM = ATTN_HEAD_DIM // 4

NEG_BIG = -1e30


def _vmem_limit(nbytes):
    return int(min(V7X_VMEM_BYTES * 7 // 8, max(32 * 1024 * 1024, nbytes * 3 // 2)))


def _params(sem, vmem_bytes):
    return pltpu.CompilerParams(dimension_semantics=sem, vmem_limit_bytes=_vmem_limit(vmem_bytes))


def _const_spec(shape):
    nd = len(shape)
    return pl.BlockSpec(shape, lambda *_: (0,) * nd)


def _rms(x, g):
    ms = jnp.mean(x * x, axis=-1, keepdims=True)
    return x * lax.rsqrt(ms + RMS_EPS) * g


def _silu(x):
    return x * (1.0 / (1.0 + jnp.exp(-x)))


def _softplus(x):
    return jnp.maximum(x, 0.0) + jnp.log1p(jnp.exp(-jnp.abs(x)))


def _dot(a, b):
    return jnp.dot(a, b, preferred_element_type=F32)


def _dot_nt(a, b):
    return lax.dot_general(a, b, (((1,), (1,)), ((), ())), preferred_element_type=F32)


def _rope(t, cos, sa, sb):
    up = pltpu.roll(t, V7X_LANES - ROT_DIM // 2, 1)
    dn = pltpu.roll(t, ROT_DIM // 2, 1)
    return t * cos + up * sa + dn * sb


def _inproj_kernel(x_ref, g_ref, w_ref, z_ref, xbc_ref, dt_ref, *, d_inner, conv_dim):
    xn = _rms(x_ref[...], g_ref[...]).astype(BF16)
    z_ref[...] = _dot(xn, w_ref[:, 0:d_inner])
    xbc_ref[...] = _dot(xn, w_ref[:, d_inner:d_inner + conv_dim])
    dt_ref[...] = _dot(xn, w_ref[:, d_inner + conv_dim:])


def ssd_in_proj(x, g, w, *, d_inner, conv_dim, tm):
    m, d = x.shape
    n = w.shape[1]
    dtw = n - d_inner - conv_dim
    vm = 2 * tm * d * 4 + 2 * d * n * 2 + 2 * tm * n * 4
    return pl.pallas_call(
        functools.partial(_inproj_kernel, d_inner=d_inner, conv_dim=conv_dim),
        grid=(m // tm,),
        in_specs=[pl.BlockSpec((tm, d), lambda i: (i, 0)), _const_spec((1, d)), _const_spec((d, n))],
        out_specs=[pl.BlockSpec((tm, d_inner), lambda i: (i, 0)),
                   pl.BlockSpec((tm, conv_dim), lambda i: (i, 0)),
                   pl.BlockSpec((tm, dtw), lambda i: (i, 0))],
        out_shape=[jax.ShapeDtypeStruct((m, d_inner), F32),
                   jax.ShapeDtypeStruct((m, conv_dim), F32),
                   jax.ShapeDtypeStruct((m, dtw), F32)],
        compiler_params=_params(("parallel",), vm),
        name="ssd_in_proj",
    )(x, g, w)


def _qproj_kernel(x_ref, g_ref, w_ref, cos_ref, sa_ref, sb_ref, q_ref, *, n_heads, scale):
    xn = _rms(x_ref[...], g_ref[...]).astype(BF16)
    cos, sa, sb = cos_ref[...], sa_ref[...], sb_ref[...]
    for h in range(n_heads):
        sl = slice(h * V7X_LANES, (h + 1) * V7X_LANES)
        t = _dot(xn, w_ref[:, sl])
        q_ref[:, sl] = (_rope(t, cos, sa, sb) * scale).astype(BF16)


def attn_q_proj(x, g, w, tables, *, tm, n_pos_blocks):
    m, d = x.shape
    n = w.shape[1]
    n_heads = n // V7X_LANES
    tab_spec = pl.BlockSpec((tm, V7X_LANES), lambda i: (i % n_pos_blocks, 0))
    vm = 2 * tm * d * 4 + 2 * d * n * 2 + 2 * tm * n * 2 + 6 * tm * V7X_LANES * 4
    return pl.pallas_call(
        functools.partial(_qproj_kernel, n_heads=n_heads, scale=ATTN_HEAD_DIM ** -0.5 * math.log2(math.e)),
        grid=(m // tm,),
        in_specs=[pl.BlockSpec((tm, d), lambda i: (i, 0)), _const_spec((1, d)), _const_spec((d, n)),
                  tab_spec, tab_spec, tab_spec],
        out_specs=pl.BlockSpec((tm, n), lambda i: (i, 0)),
        out_shape=jax.ShapeDtypeStruct((m, n), BF16),
        compiler_params=_params(("parallel",), vm),
        name="attn_q_proj",
    )(x, g, w, *tables)


def _kvproj_kernel(x_ref, g_ref, w_ref, cos_ref, sa_ref, sb_ref, k_ref, v_ref, kb_ref, vb_ref, *, n_heads):
    xn = _rms(x_ref[...], g_ref[...]).astype(BF16)
    cos, sa, sb = cos_ref[...], sa_ref[...], sb_ref[...]
    qk = n_heads * V7X_LANES
    for h in range(n_heads):
        sl = slice(h * V7X_LANES, (h + 1) * V7X_LANES)
        k = _rope(_dot(xn, w_ref[:, sl]), cos, sa, sb)
        k_ref[:, sl] = k
        kb_ref[:, sl] = k.astype(BF16)
    v = _dot(xn, w_ref[:, qk:])
    v_ref[...] = v
    vb_ref[...] = v.astype(BF16)


def shared_kv_proj(x, g, w, tables, *, tm, n_pos_blocks):
    m, d = x.shape
    n = w.shape[1]
    qk = n // 2
    n_heads = qk // V7X_LANES
    tab_spec = pl.BlockSpec((tm, V7X_LANES), lambda i: (i % n_pos_blocks, 0))
    row = lambda i: (i, 0)
    vm = 2 * tm * d * 4 + 2 * d * n * 2 + 2 * tm * n * 6 + 6 * tm * V7X_LANES * 4
    return pl.pallas_call(
        functools.partial(_kvproj_kernel, n_heads=n_heads),
        grid=(m // tm,),
        in_specs=[pl.BlockSpec((tm, d), row), _const_spec((1, d)), _const_spec((d, n)),
                  tab_spec, tab_spec, tab_spec],
        out_specs=[pl.BlockSpec((tm, qk), row)] * 4,
        out_shape=[jax.ShapeDtypeStruct((m, qk), F32), jax.ShapeDtypeStruct((m, qk), F32),
                   jax.ShapeDtypeStruct((m, qk), BF16), jax.ShapeDtypeStruct((m, qk), BF16)],
        compiler_params=_params(("parallel",), vm),
        name="shared_kv_proj",
    )(x, g, w, *tables)


def _outproj_kernel(a_ref, w_ref, g_ref, h_ref, o_ref):
    y = _dot(a_ref[...].astype(BF16), w_ref[...])
    o_ref[...] = h_ref[...] + _rms(y, g_ref[...])


def out_proj_residual(a, w, g_post, h, *, tm):
    m, k = a.shape
    d = w.shape[1]
    row = lambda i: (i, 0)
    vm = 2 * tm * k * a.dtype.itemsize + 2 * k * d * 2 + 4 * tm * d * 4
    return pl.pallas_call(
        _outproj_kernel,
        grid=(m // tm,),
        in_specs=[pl.BlockSpec((tm, k), row), _const_spec((k, d)), _const_spec((1, d)),
                  pl.BlockSpec((tm, d), row)],
        out_specs=pl.BlockSpec((tm, d), row),
        out_shape=jax.ShapeDtypeStruct((m, d), F32),
        compiler_params=_params(("parallel",), vm),
        name="out_proj_residual",
    )(a, w, g_post, h)


def _mlp_kernel(h_ref, g1_ref, wu_ref, wd_ref, g2_ref, o_ref, *, tf):
    h = h_ref[...]
    xn = _rms(h, g1_ref[...]).astype(BF16)
    d_ff = wu_ref.shape[1]
    acc = jnp.zeros(h.shape, F32)
    for f in range(d_ff // tf):
        u = jnp.maximum(_dot(xn, wu_ref[:, f * tf:(f + 1) * tf]), 0.0)
        acc = acc + _dot((u * u).astype(BF16), wd_ref[f * tf:(f + 1) * tf, :])
    o_ref[...] = h + _rms(acc, g2_ref[...])


def mlp_residual(h, g_pre, w_up, w_down, g_post, *, tm, tf=512):
    m, d = h.shape
    d_ff = w_up.shape[1]
    row = lambda i: (i, 0)
    vm = 4 * tm * d * 4 + 2 * 2 * d * d_ff * 2 + tm * d * 4 + 2 * tm * tf * 4
    return pl.pallas_call(
        functools.partial(_mlp_kernel, tf=tf),
        grid=(m // tm,),
        in_specs=[pl.BlockSpec((tm, d), row), _const_spec((1, d)), _const_spec((d, d_ff)),
                  _const_spec((d_ff, d)), _const_spec((1, d))],
        out_specs=pl.BlockSpec((tm, d), row),
        out_shape=jax.ShapeDtypeStruct((m, d), F32),
        compiler_params=_params(("parallel",), vm),
        name="mlp_residual",
    )(h, g_pre, w_up, w_down, g_post)


def _expand3(v, e):
    hi = v.astype(BF16)
    r1 = v - hi.astype(F32)
    mid = r1.astype(BF16)
    lo = (r1 - mid.astype(F32)).astype(BF16)
    return _dot(hi, e) + _dot(mid, e) + _dot(lo, e)


def _gated_group_norm(y, z, gate_norm, n_groups):
    g = y * _silu(z)
    width = g.shape[1] // n_groups
    outs = []
    for i in range(n_groups):
        gg = g[:, i * width:(i + 1) * width]
        ms = jnp.mean(gg * gg, axis=-1, keepdims=True)
        outs.append(gg * lax.rsqrt(ms + RMS_EPS) * gate_norm[:, i * width:(i + 1) * width])
    return outs


def _ssd_prompt_kernel(z_ref, xbc_ref, dtr_ref, cw_ref, cb_ref, dtb_ref, alog_ref, dexp_ref, gn_ref,
                       tri_ref, e128_ref, e64_ref,
                       g_ref, ssm_ref, conv_ref,
                       cbuf, st_t, y_s, ab_s, *, n_heads, d_inner):
    L = SSD_CHUNK
    N = SSD_D_STATE
    gn_w = SSD_N_GROUPS * N
    heads_per_group = n_heads // SSD_N_GROUPS
    c = pl.program_id(1)
    last = pl.num_programs(1) - 1
    halo = V7X_SUBLANES

    @pl.when(c == 0)
    def _():
        cbuf[0:halo, :] = jnp.zeros((halo, cbuf.shape[1]), F32)
        st_t[...] = jnp.zeros(st_t.shape, F32)

    cbuf[halo:halo + L, :] = xbc_ref[...]
    acc = cb_ref[...]
    for k in range(SSD_D_CONV):
        off = halo - (SSD_D_CONV - 1) + k
        acc = acc + cbuf[off:off + L, :] * cw_ref[k:k + 1, :]
    act = _silu(acc)
    x = act[:, :d_inner]

    @pl.when(c == last)
    def _():
        conv_ref[...] = cbuf[L:L + halo, :]

    cbuf[0:halo, :] = cbuf[L:L + halo, :]

    dt = _softplus(dtr_ref[...] + dtb_ref[...])
    a = -jnp.exp(alog_ref[...])
    acum = jnp.dot(tri_ref[...], a * dt, precision=lax.Precision.HIGHEST, preferred_element_type=F32)
    acum_last = acum[L - 1:L, :]
    ab_s[...] = _expand3(acum, e128_ref[...])
    acum_t = acum.T
    dt_t = dt.T
    w_t = (jnp.exp(acum_last - acum) * dt).T
    cd = _expand3(jnp.broadcast_to(jnp.exp(acum_last), (V7X_SUBLANES, V7X_LANES)), e64_ref[...])[0:1, :]

    li = lax.broadcasted_iota(jnp.int32, (L, L), 0)
    si = lax.broadcasted_iota(jnp.int32, (L, L), 1)
    causal = li >= si
    lo_half = lax.broadcasted_iota(jnp.int32, (L, V7X_LANES), 1) < SSD_HEAD_DIM

    for grp in range(SSD_N_GROUPS):
        b_g = act[:, d_inner + grp * N:d_inner + (grp + 1) * N]
        c_g = act[:, d_inner + gn_w + grp * N:d_inner + gn_w + (grp + 1) * N]
        cb_g = _dot_nt(c_g.astype(BF16), b_g.astype(BF16))
        bt_g = b_g.T
        for pr in range(heads_per_group // 2):
            q = grp * (heads_per_group // 2) + pr
            sl = slice(q * V7X_LANES, (q + 1) * V7X_LANES)
            xp = x[:, sl].astype(BF16)
            st_prev = st_t[:, sl]
            rhs = jnp.concatenate([xp, st_prev.astype(BF16)], axis=0)
            ys, ss = [], []
            for h in (2 * q, 2 * q + 1):
                col = ab_s[:, h * V7X_LANES:(h + 1) * V7X_LANES]
                seg = col - acum_t[h:h + 1, :]
                wm = jnp.exp(jnp.where(causal, seg, NEG_BIG)) * (cb_g * dt_t[h:h + 1, :])
                ce = c_g * jnp.exp(col)
                lhs = jnp.concatenate([wm.astype(BF16), ce.astype(BF16)], axis=1)
                ys.append(_dot(lhs, rhs))
                ss.append(_dot((bt_g * w_t[h:h + 1, :]).astype(BF16), xp))
            y_s[:, sl] = jnp.where(lo_half, ys[0], ys[1])
            st_t[:, sl] = st_prev * cd[:, sl] + jnp.where(lo_half, ss[0], ss[1])

    y = y_s[...] + x * dexp_ref[...]
    outs = _gated_group_norm(y, z_ref[...], gn_ref[...], SSD_N_GROUPS)
    width = d_inner // SSD_N_GROUPS
    for i, o in enumerate(outs):
        g_ref[:, i * width:(i + 1) * width] = o.astype(BF16)

    @pl.when(c == last)
    def _():
        for q in range(n_heads // 2):
            sl = slice(q * V7X_LANES, (q + 1) * V7X_LANES)
            ssm_ref[sl, :] = st_t[:, sl].T


def ssd_prompt(z, xbc, dtr, cw, cb, dtb, alog, dexp, gn, consts, *, bsz, seqlen, n_heads):
    m, d_inner = z.shape
    conv_dim = xbc.shape[1]
    L = SSD_CHUNK
    nc = seqlen // L
    tri, e128, e64 = consts
    blk = lambda b, c: (b * nc + c, 0)
    vm = (2 * L * (d_inner + conv_dim + V7X_LANES) * 4 + 2 * L * d_inner * 2
          + (L + 8) * conv_dim * 4 + 2 * L * d_inner * 4 + L * n_heads * V7X_LANES * 4
          + 2 * (e128.size + e64.size) * 2 + 4 * n_heads * SSD_HEAD_DIM * SSD_D_STATE * 4
          + 8 * L * conv_dim * 4)
    return pl.pallas_call(
        functools.partial(_ssd_prompt_kernel, n_heads=n_heads, d_inner=d_inner),
        grid=(bsz, nc),
        in_specs=[pl.BlockSpec((L, d_inner), blk), pl.BlockSpec((L, conv_dim), blk),
                  pl.BlockSpec((L, V7X_LANES), blk),
                  _const_spec(cw.shape), _const_spec(cb.shape), _const_spec(dtb.shape),
                  _const_spec(alog.shape), _const_spec(dexp.shape), _const_spec(gn.shape),
                  _const_spec(tri.shape), _const_spec(e128.shape), _const_spec(e64.shape)],
        out_specs=[pl.BlockSpec((L, d_inner), blk),
                   pl.BlockSpec((None, n_heads * SSD_HEAD_DIM, SSD_D_STATE), lambda b, c: (b, 0, 0)),
                   pl.BlockSpec((None, V7X_SUBLANES, conv_dim), lambda b, c: (b, 0, 0))],
        out_shape=[jax.ShapeDtypeStruct((m, d_inner), BF16),
                   jax.ShapeDtypeStruct((bsz, n_heads * SSD_HEAD_DIM, SSD_D_STATE), F32),
                   jax.ShapeDtypeStruct((bsz, V7X_SUBLANES, conv_dim), F32)],
        scratch_shapes=[pltpu.VMEM((L + V7X_SUBLANES, conv_dim), F32),
                        pltpu.VMEM((SSD_D_STATE, d_inner), F32),
                        pltpu.VMEM((L, d_inner), F32),
                        pltpu.VMEM((L, n_heads * V7X_LANES), F32)],
        compiler_params=_params(("arbitrary", "arbitrary"), vm),
        name="ssd_prompt",
    )(z, xbc, dtr, cw, cb, dtb, alog, dexp, gn, tri, e128, e64)


def _ssd_sample_conv_kernel(xbc_ref, cs_ref, dtr_ref, cw_ref, cb_ref, dtb_ref,
                            x_ref, b_ref, c_ref, dt_ref, ncs_ref, *, d_inner):
    new = xbc_ref[...]
    acc = cb_ref[...] + new * cw_ref[SSD_D_CONV - 1:SSD_D_CONV, :]
    for k in range(SSD_D_CONV - 1):
        acc = acc + cs_ref[k] * cw_ref[k:k + 1, :]
    act = _silu(acc)
    gn_w = SSD_N_GROUPS * SSD_D_STATE
    x_ref[...] = act[:, :d_inner]
    b_ref[...] = act[:, d_inner:d_inner + gn_w]
    c_ref[...] = act[:, d_inner + gn_w:]
    dt_ref[...] = _softplus(dtr_ref[...] + dtb_ref[...])
    for k in range(SSD_D_CONV - 2):
        ncs_ref[k] = cs_ref[k + 1]
    ncs_ref[SSD_D_CONV - 2] = new


def ssd_sample_conv(xbc, cs_t, dtr, cw, cb, dtb, *, d_inner):
    n, conv_dim = xbc.shape
    gn_w = SSD_N_GROUPS * SSD_D_STATE
    args = (xbc, cs_t, dtr, cw, cb, dtb)
    vm = 6 * xbc.size * 4 + 4 * cs_t.size * 4
    return pl.pallas_call(
        functools.partial(_ssd_sample_conv_kernel, d_inner=d_inner),
        grid=(1,),
        in_specs=[_const_spec(a.shape) for a in args],
        out_specs=[_const_spec((n, d_inner)), _const_spec((n, gn_w)), _const_spec((n, gn_w)),
                   _const_spec(dtr.shape), _const_spec(cs_t.shape)],
        out_shape=[jax.ShapeDtypeStruct((n, d_inner), F32), jax.ShapeDtypeStruct((n, gn_w), F32),
                   jax.ShapeDtypeStruct((n, gn_w), F32), jax.ShapeDtypeStruct(dtr.shape, F32),
                   jax.ShapeDtypeStruct(cs_t.shape, F32)],
        compiler_params=_params(("arbitrary",), vm),
        name="ssd_sample_conv",
    )(*args)


def _ssd_sample_state_kernel(st_ref, xt_ref, dtb_ref, dtr_ref, alog_ref, b_ref, c_ref,
                             nst_ref, y_ref, *, sb, n_heads):
    heads_per_group = n_heads // SSD_N_GROUPS
    N = SSD_D_STATE
    P = SSD_HEAD_DIM
    a_rep = -jnp.exp(alog_ref[...])
    for s in range(sb):
        dec = jnp.exp(a_rep * dtb_ref[s])
        xdt_t = xt_ref[s] * dtr_ref[s]
        for grp in range(SSD_N_GROUPS):
            b_row = b_ref[s][:, grp * N:(grp + 1) * N]
            c_row = c_ref[s][:, grp * N:(grp + 1) * N]
            news = []
            for r in range(heads_per_group):
                h = grp * heads_per_group + r
                new = st_ref[s, h] * dec[h:h + 1, :] + xdt_t[:, h:h + 1] * b_row
                nst_ref[s, h] = new
                news.append(new)
            hg = jnp.concatenate(news, axis=0).astype(BF16)
            c8 = jnp.broadcast_to(c_row, (V7X_SUBLANES, N)).astype(BF16)
            yg = _dot_nt(c8, hg)
            y_ref[s, :, grp * heads_per_group * P:(grp + 1) * heads_per_group * P] = yg[0:1, :]


def ssd_sample_state(state, x_t, dt_bcast, dt_row, alog_rep, b_in, c_in, *, sb):
    n, n_heads, P, N = state.shape
    d_inner = n_heads * P
    gn_w = b_in.shape[-1]
    blk4 = lambda i: (i, 0, 0, 0)
    blk3 = lambda i: (i, 0, 0)
    vm = 4 * sb * n_heads * P * N * 4 + 4 * sb * (P * V7X_LANES + n_heads * V7X_LANES) * 4
    return pl.pallas_call(
        functools.partial(_ssd_sample_state_kernel, sb=sb, n_heads=n_heads),
        grid=(n // sb,),
        in_specs=[pl.BlockSpec((sb, n_heads, P, N), blk4),
                  pl.BlockSpec((sb, P, n_heads), blk3),
                  pl.BlockSpec((sb, n_heads, V7X_LANES), blk3),
                  pl.BlockSpec((sb, 1, n_heads), blk3),
                  _const_spec(alog_rep.shape),
                  pl.BlockSpec((sb, 1, gn_w), blk3),
                  pl.BlockSpec((sb, 1, gn_w), blk3)],
        out_specs=[pl.BlockSpec((sb, n_heads, P, N), blk4),
                   pl.BlockSpec((sb, 1, d_inner), blk3)],
        out_shape=[jax.ShapeDtypeStruct(state.shape, F32),
                   jax.ShapeDtypeStruct((n, 1, d_inner), F32)],
        compiler_params=_params(("parallel",), vm),
        name="ssd_sample_state",
    )(state, x_t, dt_bcast, dt_row, alog_rep, b_in, c_in)


def _ssd_sample_gate_kernel(y_ref, x_ref, z_ref, dexp_ref, gn_ref, g_ref):
    y = y_ref[...] + x_ref[...] * dexp_ref[...]
    outs = _gated_group_norm(y, z_ref[...], gn_ref[...], SSD_N_GROUPS)
    width = y.shape[1] // SSD_N_GROUPS
    for i, o in enumerate(outs):
        g_ref[:, i * width:(i + 1) * width] = o.astype(BF16)


def ssd_sample_gate(y, x, z, dexp, gn):
    args = (y, x, z, dexp, gn)
    return pl.pallas_call(
        _ssd_sample_gate_kernel,
        grid=(1,),
        in_specs=[_const_spec(a.shape) for a in args],
        out_specs=_const_spec(y.shape),
        out_shape=jax.ShapeDtypeStruct(y.shape, BF16),
        compiler_params=_params(("arbitrary",), 8 * y.size * 4),
        name="ssd_sample_gate",
    )(*args)


def _diff_lambda(lam_ref, lam_init):
    v = lam_ref[...]
    d1 = jnp.sum(v[0:1, :] * v[1:2, :], axis=-1, keepdims=True)
    d2 = jnp.sum(v[2:3, :] * v[3:4, :], axis=-1, keepdims=True)
    return jnp.exp(d1) - jnp.exp(d2) + lam_init


V_ROWS_PAD = 16


def _flash_kernel(lam_ref, q_ref, k_ref, vt_ref, subln_ref, o_ref, *scratch, t, hb, unroll, lam_init):
    qi = pl.program_id(2)
    m_s, acc_s = scratch[:hb], scratch[hb:]
    q2 = []
    for s in range(hb):
        q = q_ref[:, s * V7X_LANES:(s + 1) * V7X_LANES]
        lane = lax.broadcasted_iota(jnp.int32, q.shape, 1)
        zero = jnp.zeros_like(q)
        q2.append(jnp.concatenate([jnp.where(lane < ATTN_HEAD_DIM, q, zero),
                                   jnp.where(lane >= ATTN_HEAD_DIM, q, zero)], axis=0))
        m_s[s][...] = jnp.full(m_s[s].shape, NEG_BIG, F32)
        acc_s[s][...] = jnp.zeros(acc_s[s].shape, F32)

    def scores(j):
        start = pl.multiple_of(j * t, t)
        return tuple(_dot_nt(k_ref[pl.ds(start, t), s * V7X_LANES:(s + 1) * V7X_LANES], q2[s])
                     for s in range(hb))

    def accumulate(j, scs, masked):
        start = pl.multiple_of(j * t, t)
        for s in range(hb):
            sc = scs[s]
            if masked:
                kv = lax.broadcasted_iota(jnp.int32, sc.shape, 0)
                qq = lax.broadcasted_iota(jnp.int32, sc.shape, 1)
                qq = jnp.where(qq >= t, qq - t, qq)
                sc = jnp.where(kv <= qq, sc, NEG_BIG)
            m_old = m_s[s][...]
            m_new = jnp.maximum(m_old, jnp.max(sc, axis=0, keepdims=True))
            alpha = jnp.exp2(m_old - m_new)
            p = jnp.exp2(sc - m_new).astype(BF16)
            acc_s[s][...] = alpha * acc_s[s][...] + _dot(vt_ref[s, :, pl.ds(start, t)], p)
            m_s[s][...] = m_new

    def run_blocks(j0, n, masked_last):
        scs = scores(j0)
        for u in range(n):
            nxt = scores(j0 + u + 1) if u + 1 < n else None
            accumulate(j0 + u, scs, masked_last and u == n - 1)
            scs = nxt

    def body(g, carry):
        run_blocks(g * unroll, unroll, False)
        return carry

    n_groups = qi // unroll
    lax.fori_loop(0, n_groups, body, 0)
    rem = qi - n_groups * unroll
    for r in range(unroll):
        @pl.when(rem == r)
        def _(r=r):
            run_blocks(n_groups * unroll, r + 1, True)

    lam = _diff_lambda(lam_ref, lam_init)
    gain = subln_ref[...] * (1.0 - lam_init)
    for s in range(hb):
        acc = acc_s[s][...]
        acc = acc[:V7X_LANES, :] * (1.0 / acc[V7X_LANES:V7X_LANES + 1, :])
        o_t = acc[:, :t] - lam * acc[:, t:]
        ms = jnp.mean(o_t * o_t, axis=0, keepdims=True)
        o_t = o_t * lax.rsqrt(ms + RMS_EPS)
        o_ref[:, s * V7X_LANES:(s + 1) * V7X_LANES] = (o_t.T * gain).astype(BF16)


def diff_attn_prompt(lam_vecs, q, k, v_t, subln, *, bsz, seqlen, n_heads, lam_init, t=256, hb=4, unroll=2):
    nq = seqlen // t
    vr = v_t.shape[2]
    w = hb * V7X_LANES
    vm = (2 * (2 * t * w * 2 + seqlen * w * 2 + hb * vr * seqlen * 2) + hb * (vr + 8) * 2 * t * 4
          + hb * 4 * t * 2 * t * 4)
    return pl.pallas_call(
        functools.partial(_flash_kernel, t=t, hb=hb, unroll=unroll, lam_init=lam_init),
        grid=(bsz, n_heads // hb, nq),
        in_specs=[_const_spec(lam_vecs.shape),
                  pl.BlockSpec((t, w), lambda b, h, i: (b * nq + i, h)),
                  pl.BlockSpec((seqlen, w), lambda b, h, i: (b, h)),
                  pl.BlockSpec((None, hb, vr, seqlen), lambda b, h, i: (b, h, 0, 0)),
                  _const_spec(subln.shape)],
        out_specs=pl.BlockSpec((t, w), lambda b, h, i: (b * nq + i, h)),
        out_shape=jax.ShapeDtypeStruct(q.shape, BF16),
        scratch_shapes=[pltpu.VMEM((1, 2 * t), F32)] * hb + [pltpu.VMEM((vr, 2 * t), F32)] * hb,
        compiler_params=_params(("parallel", "parallel", "arbitrary"), vm),
        name="diff_attn_prompt",
    )(lam_vecs, q, k, v_t, subln)


def _decode_kernel(pt_ref, lam_ref, q_ref, kn_ref, vn_ref, subln_ref, *rest, n_pages, lam_init):
    k_refs = rest[:n_pages]
    v_refs = rest[n_pages:2 * n_pages]
    o_ref = rest[2 * n_pages]
    del pt_ref
    _, page, n_heads, hd = k_refs[0].shape
    rows = 2 * n_heads
    prow = page * n_heads
    q8 = q_ref[0].astype(F32)
    lane = lax.broadcasted_iota(jnp.int32, q8.shape, 1)
    q16 = jnp.concatenate([jnp.where(lane < ATTN_HEAD_DIM, q8, 0.0),
                           jnp.where(lane >= ATTN_HEAD_DIM, q8, 0.0)], axis=0)
    q16b = q16.astype(BF16)
    r_i = lax.broadcasted_iota(jnp.int32, (rows, prow), 0)
    c_i = lax.broadcasted_iota(jnp.int32, (rows, prow), 1)
    own = (r_i % n_heads) == (c_i % n_heads)

    s_parts = []
    for p in range(n_pages):
        kp = k_refs[p][0].reshape(prow, hd).astype(BF16)
        s_parts.append(jnp.where(own, _dot_nt(q16b, kp), NEG_BIG))
    kn = kn_ref[0]
    kn2 = jnp.concatenate([kn, kn], axis=0)
    s_new = jnp.sum(q16 * kn2, axis=1, keepdims=True)
    m = s_new
    for sp in s_parts:
        m = jnp.maximum(m, jnp.max(sp, axis=1, keepdims=True))
    p_new = jnp.exp2(s_new - m)
    vn = vn_ref[0]
    acc = p_new * jnp.concatenate([vn, vn], axis=0)
    denom = p_new
    for p in range(n_pages):
        pp = jnp.exp2(s_parts[p] - m)
        denom = denom + jnp.sum(pp, axis=1, keepdims=True)
        acc = acc + _dot(pp.astype(BF16), v_refs[p][0].reshape(prow, hd).astype(BF16))

    lam = _diff_lambda(lam_ref, lam_init)
    acc = acc * (1.0 / denom)
    d = acc[:n_heads, :] - lam * acc[n_heads:, :]
    ms = jnp.mean(d * d, axis=1, keepdims=True)
    o_ref[0] = d * lax.rsqrt(ms + RMS_EPS) * (subln_ref[...] * (1.0 - lam_init))


def diff_attn_decode(page_table, lam_vecs, q, k_new, v_new, subln, cache_k, cache_v, *, lam_init):
    n, n_pages = page_table.shape
    _, page, n_heads, hd = cache_k.shape
    row = lambda b, pt: (b, 0, 0)
    const2 = lambda b, pt: (0, 0)
    page_specs = [pl.BlockSpec((1, page, n_heads, hd),
                               functools.partial(lambda b, pt, i: (pt[b, i], 0, 0, 0), i=i))
                  for i in range(n_pages)]
    vm = 2 * 2 * n_pages * page * n_heads * hd * 4 + 4 * 2 * n_heads * page * n_heads * n_pages * 4
    grid_spec = pltpu.PrefetchScalarGridSpec(
        num_scalar_prefetch=1,
        grid=(n,),
        in_specs=[pl.BlockSpec(lam_vecs.shape, const2),
                  pl.BlockSpec((1, n_heads, hd), row), pl.BlockSpec((1, n_heads, hd), row),
                  pl.BlockSpec((1, n_heads, hd), row), pl.BlockSpec(subln.shape, const2)]
                 + page_specs + page_specs,
        out_specs=pl.BlockSpec((1, n_heads, hd), row),
    )
    return pl.pallas_call(
        functools.partial(_decode_kernel, n_pages=n_pages, lam_init=lam_init),
        grid_spec=grid_spec,
        out_shape=jax.ShapeDtypeStruct((n, n_heads, hd), F32),
        compiler_params=_params(("parallel",), vm),
        name="diff_attn_decode",
    )(page_table, lam_vecs, q, k_new, v_new, subln, *([cache_k] * n_pages), *([cache_v] * n_pages))


def _rope_tables(pos):
    half = ROT_DIM // 2
    inv = ROPE_THETA ** (-jnp.arange(0, ROT_DIM, 2, dtype=F32) / ROT_DIM)
    ang = pos.astype(F32)[:, None] * inv[None, :]
    cos, sin = jnp.cos(ang), jnp.sin(ang)
    n = pos.shape[0]
    pad = jnp.zeros((n, ATTN_HEAD_DIM - ROT_DIM), F32)
    zeros = jnp.zeros((n, half), F32)
    c_map = jnp.concatenate([cos, cos, pad + 1.0], axis=1)
    sa_map = jnp.concatenate([-sin, zeros, pad], axis=1)
    sb_map = jnp.concatenate([zeros, sin, pad], axis=1)
    return tuple(jnp.concatenate([t, t], axis=1) for t in (c_map, sa_map, sb_map))


def _row(v, width=None):
    v = v.astype(F32).reshape(1, -1)
    if width is not None and v.shape[1] < width:
        v = jnp.pad(v, ((0, 0), (0, width - v.shape[1])))
    return v


def kernel(x_prompt, x_sample, state_ssm, state_conv, cache_k, cache_v, page_table, ssd_norm_pre, ssd_norm_post, ssd_w_in, ssd_conv_w, ssd_conv_b, ssd_dt_bias, ssd_a_log, ssd_d, ssd_gate_norm, ssd_w_out, mlp_norm_pre, mlp_norm_post, mlp_w_up, mlp_w_down, kv_norm, w_kv, attn_norm_pre, attn_norm_post, attn_w_q, attn_lambda_q1, attn_lambda_k1, attn_lambda_q2, attn_lambda_k2, attn_subln, attn_w_o):
    bsz, seqlen, d_model = x_prompt.shape
    n_dec = x_sample.shape[0]
    n_ssd = ssd_w_in.shape[0]
    depth = mlp_w_up.shape[0]
    n_heads_ssd = ssd_a_log.shape[1]
    d_inner = n_heads_ssd * SSD_HEAD_DIM
    conv_dim = ssd_conv_w.shape[2]
    qk_dim = attn_w_q.shape[2]
    n_heads = qk_dim // V7X_LANES
    m_p = bsz * seqlen

    hp = x_prompt.reshape(m_p, d_model)
    hs = x_sample.reshape(n_dec, d_model)

    tm_p = 256
    tm_mlp = 512
    tm_s = n_dec

    L = SSD_CHUNK
    tri = (jnp.arange(L)[:, None] >= jnp.arange(L)[None, :]).astype(F32)
    hrow = jnp.arange(V7X_LANES)[:, None]
    e128 = (hrow == (jnp.arange(n_heads_ssd * V7X_LANES)[None, :] // V7X_LANES)).astype(BF16)
    e64 = (hrow == (jnp.arange(d_inner)[None, :] // SSD_HEAD_DIM)).astype(BF16)

    tab_p = _rope_tables(jnp.arange(seqlen))
    past_len = page_table.shape[1] * cache_k.shape[1]
    tab_s = tuple(jnp.broadcast_to(t, (n_dec, V7X_LANES)) for t in _rope_tables(jnp.full((1,), past_len)))

    ssm_p, conv_p, ssm_s, conv_s = [], [], [], []
    k_p = v_p = k_s = v_s = None
    kb_p = vt_p = None

    for i in range(depth):
        if i < n_ssd:
            j = i
            w_in = ssd_w_in[j]
            w_in_b = jnp.concatenate(
                [w_in[:, :d_inner + conv_dim],
                 jnp.pad(w_in[:, d_inner + conv_dim:], ((0, 0), (0, V7X_LANES - n_heads_ssd)))], axis=1).astype(BF16)
            w_out_b = ssd_w_out[j].astype(BF16)
            g_pre = _row(ssd_norm_pre[j])
            g_post = _row(ssd_norm_post[j])
            cw = ssd_conv_w[j].astype(F32)
            cb = _row(ssd_conv_b[j])
            dtb = _row(ssd_dt_bias[j], V7X_LANES)
            alog = _row(ssd_a_log[j], V7X_LANES)
            dexp = jnp.repeat(ssd_d[j].astype(F32), SSD_HEAD_DIM).reshape(1, d_inner)
            gn = _row(ssd_gate_norm[j])

            z, xbc, dtr = ssd_in_proj(hp, g_pre, w_in_b, d_inner=d_inner, conv_dim=conv_dim, tm=tm_p)
            g, sp, cp = ssd_prompt(z, xbc, dtr, cw, cb, dtb, alog, dexp, gn, (tri, e128, e64),
                                   bsz=bsz, seqlen=seqlen, n_heads=n_heads_ssd)
            hp = out_proj_residual(g, w_out_b, g_post, hp, tm=tm_p)
            ssm_p.append(sp.reshape(bsz, n_heads_ssd, SSD_HEAD_DIM, SSD_D_STATE))
            conv_p.append(cp[:, V7X_SUBLANES - (SSD_D_CONV - 1):, :])

            zs, xbcs, dtrs = ssd_in_proj(hs, g_pre, w_in_b, d_inner=d_inner, conv_dim=conv_dim, tm=tm_s)
            cs_t = jnp.transpose(state_conv[j].astype(F32), (1, 0, 2))
            xs, bs_, cs_, dts, ncs = ssd_sample_conv(xbcs, cs_t, dtrs, cw, cb, dtb, d_inner=d_inner)
            x_t = jnp.transpose(xs.reshape(n_dec, n_heads_ssd, SSD_HEAD_DIM), (0, 2, 1))
            dt_h = dts[:, :n_heads_ssd]
            dt_bcast = jnp.broadcast_to(dt_h[:, :, None], (n_dec, n_heads_ssd, V7X_LANES))
            alog_rep = jnp.broadcast_to(ssd_a_log[j].astype(F32)[:, None], (n_heads_ssd, V7X_LANES))
            nst, ys = ssd_sample_state(state_ssm[j].astype(F32), x_t, dt_bcast, dt_h[:, None, :], alog_rep,
                                       bs_[:, None, :], cs_[:, None, :], sb=4)
            gs = ssd_sample_gate(ys.reshape(n_dec, d_inner), xs, zs, dexp, gn)
            hs = out_proj_residual(gs, w_out_b, g_post, hs, tm=tm_s)
            ssm_s.append(nst.astype(state_ssm.dtype))
            conv_s.append(jnp.transpose(ncs, (1, 0, 2)).astype(state_conv.dtype))
        else:
            j = i - n_ssd
            if j == 0:
                w_kv_b = w_kv.astype(BF16)
                g_kv = _row(kv_norm)
                k_p, v_p, kb_p, vb_p = shared_kv_proj(hp, g_kv, w_kv_b, tab_p, tm=tm_p, n_pos_blocks=seqlen // tm_p)
                k_s, v_s, _, _ = shared_kv_proj(hs, g_kv, w_kv_b, tab_s, tm=tm_s, n_pos_blocks=1)
                vt_p = jnp.transpose(vb_p.reshape(bsz, seqlen, n_heads, V7X_LANES), (0, 2, 3, 1))
                ones_rows = jnp.zeros((bsz, n_heads, V_ROWS_PAD, seqlen), BF16).at[:, :, 0, :].set(1.0)
                vt_p = jnp.concatenate([vt_p, ones_rows], axis=2)
            lam_init = 0.8 - 0.6 * math.exp(-0.3 * i)
            lam_vecs = jnp.pad(
                jnp.stack([attn_lambda_q1[j], attn_lambda_k1[j], attn_lambda_q2[j], attn_lambda_k2[j]]).astype(F32),
                ((0, V7X_SUBLANES - 4), (0, V7X_LANES - ATTN_HEAD_DIM)))
            g_pre = _row(attn_norm_pre[j])
            g_post = _row(attn_norm_post[j])
            w_q_b = attn_w_q[j].astype(BF16)
            w_o_b = attn_w_o[j].astype(BF16)
            subln = _row(attn_subln[j])

            qp = attn_q_proj(hp, g_pre, w_q_b, tab_p, tm=tm_p, n_pos_blocks=seqlen // tm_p)
            op = diff_attn_prompt(lam_vecs, qp, kb_p, vt_p, subln, bsz=bsz, seqlen=seqlen, n_heads=n_heads,
                                  lam_init=lam_init)
            hp = out_proj_residual(op, w_o_b, g_post, hp, tm=tm_p)

            qs = attn_q_proj(hs, g_pre, w_q_b, tab_s, tm=tm_s, n_pos_blocks=1)
            heads = lambda a: a.reshape(n_dec, n_heads, V7X_LANES)
            os_ = diff_attn_decode(page_table, lam_vecs, heads(qs), heads(k_s), heads(v_s),
                                   subln, cache_k, cache_v, lam_init=lam_init)
            hs = out_proj_residual(os_.reshape(n_dec, qk_dim), w_o_b, g_post, hs, tm=tm_s)

        g1 = _row(mlp_norm_pre[i])
        g2 = _row(mlp_norm_post[i])
        wu = mlp_w_up[i].astype(BF16)
        wd = mlp_w_down[i].astype(BF16)
        hp = mlp_residual(hp, g1, wu, wd, g2, tm=tm_mlp)
        hs = mlp_residual(hs, g1, wu, wd, g2, tm=tm_s)

    y_prompt = hp.reshape(bsz, seqlen, d_model)
    y_sample = hs.reshape(n_dec, 1, d_model)
    p_k = k_p.reshape(bsz, seqlen, n_heads, V7X_LANES)
    p_v = v_p.reshape(bsz, seqlen, n_heads, V7X_LANES)
    s_k = k_s.reshape(n_dec, 1, n_heads, V7X_LANES)
    s_v = v_s.reshape(n_dec, 1, n_heads, V7X_LANES)
    return (y_prompt, y_sample, jnp.stack(ssm_p), jnp.stack(conv_p), p_k, p_v,
            jnp.stack(ssm_s), jnp.stack(conv_s), s_k, s_v)
```

```python
import functools
import math

import jax
import jax.numpy as jnp
from jax import lax
from jax.experimental import pallas as pl
from jax.experimental.pallas import tpu as pltpu

F32 = jnp.float32
BF16 = jnp.bfloat16

RMS_EPS = 1e-6
ROPE_THETA = 500000.0

V7X_LANES = 128
V7X_SUBLANES = 8
V7X_VMEM_BYTES = 64 * 1024 * 1024

SSD_HEAD_DIM = 64
SSD_N_GROUPS = 4
SSD_D_STATE = 128
SSD_D_CONV = 4
SSD_CHUNK = 128
ATTN_HEAD_DIM = 64
ROT_DIM = ATTN_HEAD_DIM // 4

NEG_BIG = -1e30


def _vmem_limit(nbytes):
    return int(min(V7X_VMEM_BYTES * 7 // 8, max(32 * 1024 * 1024, nbytes * 3 // 2)))


def _params(sem, vmem_bytes):
    return pltpu.CompilerParams(dimension_semantics=sem, vmem_limit_bytes=_vmem_limit(vmem_bytes))


def _const_spec(shape):
    nd = len(shape)
    return pl.BlockSpec(shape, lambda *_: (0,) * nd)


def _rms(x, g):
    ms = jnp.mean(x * x, axis=-1, keepdims=True)
    return x * lax.rsqrt(ms + RMS_EPS) * g


def _silu(x):
    return x * (1.0 / (1.0 + jnp.exp(-x)))


def _softplus(x):
    return jnp.maximum(x, 0.0) + jnp.log1p(jnp.exp(-jnp.abs(x)))


def _dot(a, b):
    return jnp.dot(a, b, preferred_element_type=F32)


def _dot_nt(a, b):
    return lax.dot_general(a, b, (((1,), (1,)), ((), ())), preferred_element_type=F32)


def _rope(t, cos, sa, sb):
    up = pltpu.roll(t, V7X_LANES - ROT_DIM // 2, 1)
    dn = pltpu.roll(t, ROT_DIM // 2, 1)
    return t * cos + up * sa + dn * sb


def _inproj_kernel(x_ref, g_ref, w_ref, z_ref, xbc_ref, dt_ref, *, d_inner, conv_dim):
    xn = _rms(x_ref[...], g_ref[...]).astype(BF16)
    z_ref[...] = _dot(xn, w_ref[:, 0:d_inner])
    xbc_ref[...] = _dot(xn, w_ref[:, d_inner:d_inner + conv_dim])
    dt_ref[...] = _dot(xn, w_ref[:, d_inner + conv_dim:])


def ssd_in_proj(x, g, w, *, d_inner, conv_dim, tm):
    m, d = x.shape
    n = w.shape[1]
    dtw = n - d_inner - conv_dim
    vm = 2 * tm * d * 4 + 2 * d * n * 2 + 2 * tm * n * 4
    return pl.pallas_call(
        functools.partial(_inproj_kernel, d_inner=d_inner, conv_dim=conv_dim),
        grid=(m // tm,),
        in_specs=[pl.BlockSpec((tm, d), lambda i: (i, 0)), _const_spec((1, d)), _const_spec((d, n))],
        out_specs=[pl.BlockSpec((tm, d_inner), lambda i: (i, 0)),
                   pl.BlockSpec((tm, conv_dim), lambda i: (i, 0)),
                   pl.BlockSpec((tm, dtw), lambda i: (i, 0))],
        out_shape=[jax.ShapeDtypeStruct((m, d_inner), F32),
                   jax.ShapeDtypeStruct((m, conv_dim), F32),
                   jax.ShapeDtypeStruct((m, dtw), F32)],
        compiler_params=_params(("parallel",), vm),
        name="ssd_in_proj",
    )(x, g, w)


def _qproj_kernel(x_ref, g_ref, w_ref, cos_ref, sa_ref, sb_ref, q_ref, *, n_heads, scale):
    xn = _rms(x_ref[...], g_ref[...]).astype(BF16)
    cos, sa, sb = cos_ref[...], sa_ref[...], sb_ref[...]
    for h in range(n_heads):
        sl = slice(h * V7X_LANES, (h + 1) * V7X_LANES)
        t = _dot(xn, w_ref[:, sl])
        q_ref[:, sl] = (_rope(t, cos, sa, sb) * scale).astype(BF16)


def attn_q_proj(x, g, w, tables, *, tm, n_pos_blocks):
    m, d = x.shape
    n = w.shape[1]
    n_heads = n // V7X_LANES
    tab_spec = pl.BlockSpec((tm, V7X_LANES), lambda i: (i % n_pos_blocks, 0))
    vm = 2 * tm * d * 4 + 2 * d * n * 2 + 2 * tm * n * 2 + 6 * tm * V7X_LANES * 4
    return pl.pallas_call(
        functools.partial(_qproj_kernel, n_heads=n_heads, scale=ATTN_HEAD_DIM ** -0.5 * math.log2(math.e)),
        grid=(m // tm,),
        in_specs=[pl.BlockSpec((tm, d), lambda i: (i, 0)), _const_spec((1, d)), _const_spec((d, n)),
                  tab_spec, tab_spec, tab_spec],
        out_specs=pl.BlockSpec((tm, n), lambda i: (i, 0)),
        out_shape=jax.ShapeDtypeStruct((m, n), BF16),
        compiler_params=_params(("parallel",), vm),
        name="attn_q_proj",
    )(x, g, w, *tables)


def _kvproj_kernel(x_ref, g_ref, w_ref, cos_ref, sa_ref, sb_ref, k_ref, v_ref, kb_ref, vb_ref, *, n_heads):
    xn = _rms(x_ref[...], g_ref[...]).astype(BF16)
    cos, sa, sb = cos_ref[...], sa_ref[...], sb_ref[...]
    qk = n_heads * V7X_LANES
    for h in range(n_heads):
        sl = slice(h * V7X_LANES, (h + 1) * V7X_LANES)
        k = _rope(_dot(xn, w_ref[:, sl]), cos, sa, sb)
        k_ref[:, sl] = k
        kb_ref[:, sl] = k.astype(BF16)
    v = _dot(xn, w_ref[:, qk:])
    v_ref[...] = v
    vb_ref[...] = v.astype(BF16)


def shared_kv_proj(x, g, w, tables, *, tm, n_pos_blocks):
    m, d = x.shape
    n = w.shape[1]
    qk = n // 2
    n_heads = qk // V7X_LANES
    tab_spec = pl.BlockSpec((tm, V7X_LANES), lambda i: (i % n_pos_blocks, 0))
    row = lambda i: (i, 0)
    vm = 2 * tm * d * 4 + 2 * d * n * 2 + 2 * tm * n * 6 + 6 * tm * V7X_LANES * 4
    return pl.pallas_call(
        functools.partial(_kvproj_kernel, n_heads=n_heads),
        grid=(m // tm,),
        in_specs=[pl.BlockSpec((tm, d), row), _const_spec((1, d)), _const_spec((d, n)),
                  tab_spec, tab_spec, tab_spec],
        out_specs=[pl.BlockSpec((tm, qk), row)] * 4,
        out_shape=[jax.ShapeDtypeStruct((m, qk), F32), jax.ShapeDtypeStruct((m, qk), F32),
                   jax.ShapeDtypeStruct((m, qk), BF16), jax.ShapeDtypeStruct((m, qk), BF16)],
        compiler_params=_params(("parallel",), vm),
        name="shared_kv_proj",
    )(x, g, w, *tables)


def _outproj_kernel(a_ref, w_ref, g_ref, h_ref, o_ref):
    y = _dot(a_ref[...].astype(BF16), w_ref[...])
    o_ref[...] = h_ref[...] + _rms(y, g_ref[...])


def out_proj_residual(a, w, g_post, h, *, tm):
    m, k = a.shape
    d = w.shape[1]
    row = lambda i: (i, 0)
    vm = 2 * tm * k * a.dtype.itemsize + 2 * k * d * 2 + 4 * tm * d * 4
    return pl.pallas_call(
        _outproj_kernel,
        grid=(m // tm,),
        in_specs=[pl.BlockSpec((tm, k), row), _const_spec((k, d)), _const_spec((1, d)),
                  pl.BlockSpec((tm, d), row)],
        out_specs=pl.BlockSpec((tm, d), row),
        out_shape=jax.ShapeDtypeStruct((m, d), F32),
        compiler_params=_params(("parallel",), vm),
        name="out_proj_residual",
    )(a, w, g_post, h)


def _mlp_kernel(h_ref, g1_ref, wu_ref, wd_ref, g2_ref, o_ref, *, tf):
    h = h_ref[...]
    xn = _rms(h, g1_ref[...]).astype(BF16)
    d_ff = wu_ref.shape[1]
    acc = jnp.zeros(h.shape, F32)
    for f in range(d_ff // tf):
        u = jnp.maximum(_dot(xn, wu_ref[:, f * tf:(f + 1) * tf]), 0.0)
        acc = acc + _dot((u * u).astype(BF16), wd_ref[f * tf:(f + 1) * tf, :])
    o_ref[...] = h + _rms(acc, g2_ref[...])


def mlp_residual(h, g_pre, w_up, w_down, g_post, *, tm, tf=512):
    m, d = h.shape
    d_ff = w_up.shape[1]
    row = lambda i: (i, 0)
    vm = 4 * tm * d * 4 + 2 * 2 * d * d_ff * 2 + tm * d * 4 + 2 * tm * tf * 4
    return pl.pallas_call(
        functools.partial(_mlp_kernel, tf=tf),
        grid=(m // tm,),
        in_specs=[pl.BlockSpec((tm, d), row), _const_spec((1, d)), _const_spec((d, d_ff)),
                  _const_spec((d_ff, d)), _const_spec((1, d))],
        out_specs=pl.BlockSpec((tm, d), row),
        out_shape=jax.ShapeDtypeStruct((m, d), F32),
        compiler_params=_params(("parallel",), vm),
        name="mlp_residual",
    )(h, g_pre, w_up, w_down, g_post)


def _expand3(v, e):
    hi = v.astype(BF16)
    r1 = v - hi.astype(F32)
    mid = r1.astype(BF16)
    lo = (r1 - mid.astype(F32)).astype(BF16)
    return _dot(hi, e) + _dot(mid, e) + _dot(lo, e)


def _gated_group_norm(y, z, gate_norm, n_groups):
    g = y * _silu(z)
    width = g.shape[1] // n_groups
    outs = []
    for i in range(n_groups):
        gg = g[:, i * width:(i + 1) * width]
        ms = jnp.mean(gg * gg, axis=-1, keepdims=True)
        outs.append(gg * lax.rsqrt(ms + RMS_EPS) * gate_norm[:, i * width:(i + 1) * width])
    return outs


def _ssd_prompt_kernel(z_ref, xbc_ref, dtr_ref, cw_ref, cb_ref, dtb_ref, alog_ref, dexp_ref, gn_ref,
                       tri_ref, e64_ref,
                       g_ref, ssm_ref, conv_ref,
                       cbuf, st_t, y_s, *, n_heads, d_inner):
    L = SSD_CHUNK
    N = SSD_D_STATE
    gn_w = SSD_N_GROUPS * N
    heads_per_group = n_heads // SSD_N_GROUPS
    c = pl.program_id(1)
    last = pl.num_programs(1) - 1
    halo = V7X_SUBLANES

    @pl.when(c == 0)
    def _():
        cbuf[0:halo, :] = jnp.zeros((halo, cbuf.shape[1]), F32)
        st_t[...] = jnp.zeros(st_t.shape, F32)

    cbuf[halo:halo + L, :] = xbc_ref[...]
    xw = cbuf[...]
    acc = cb_ref[...] + xw[halo:, :] * cw_ref[SSD_D_CONV - 1:SSD_D_CONV, :]
    for k in range(1, SSD_D_CONV):
        acc = acc + pltpu.roll(xw, k, 0)[halo:, :] * cw_ref[SSD_D_CONV - 1 - k:SSD_D_CONV - k, :]
    act = _silu(acc)
    x = act[:, :d_inner]
    cbuf[0:halo, :] = xw[L:, :]

    dt = _softplus(dtr_ref[...] + dtb_ref[...])
    a = -jnp.exp(alog_ref[...])
    acum = jnp.dot(tri_ref[...], a * dt, precision=lax.Precision.HIGHEST, preferred_element_type=F32)
    acum_last = acum[L - 1:L, :]
    acum_t = acum.T
    cd = _expand3(jnp.broadcast_to(jnp.exp(acum_last), (V7X_SUBLANES, V7X_LANES)), e64_ref[...])[0:1, :]
    e64 = e64_ref[...]
    dt_x = _dot(dt.astype(BF16), e64)
    w_x = _dot((jnp.exp(acum_last - acum) * dt).astype(BF16), e64)
    e1_x = _dot(jnp.exp(acum).astype(BF16), e64)
    xdt = (x * dt_x).astype(BF16)
    xs = (x * w_x).astype(BF16)

    li = lax.broadcasted_iota(jnp.int32, (L, L), 0)
    si = lax.broadcasted_iota(jnp.int32, (L, L), 1)
    causal = li >= si
    lo_half = lax.broadcasted_iota(jnp.int32, (L, V7X_LANES), 1) < SSD_HEAD_DIM

    for grp in range(SSD_N_GROUPS):
        b_g = act[:, d_inner + grp * N:d_inner + (grp + 1) * N]
        c_g = act[:, d_inner + gn_w + grp * N:d_inner + gn_w + (grp + 1) * N].astype(BF16)
        cb_g = jnp.where(causal, _dot_nt(c_g, b_g.astype(BF16)), 0.0)
        bt_g = b_g.T.astype(BF16)
        for pr in range(heads_per_group // 2):
            q = grp * (heads_per_group // 2) + pr
            sl = slice(q * V7X_LANES, (q + 1) * V7X_LANES)
            st_prev = st_t[:, sl]
            y_off = _dot(c_g, st_prev.astype(BF16)) * e1_x[:, sl]
            st_t[:, sl] = st_prev * cd[:, sl] + _dot(bt_g, xs[:, sl])
            xp = xdt[:, sl]
            ys = []
            for h in (2 * q, 2 * q + 1):
                seg = jnp.minimum(jnp.broadcast_to(acum[:, h:h + 1], (L, L)) - acum_t[h:h + 1, :], 0.0)
                ys.append(_dot((jnp.exp(seg) * cb_g).astype(BF16), xp))
            y_s[:, sl] = jnp.where(lo_half, ys[0], ys[1]) + y_off

    y = y_s[...] + x * dexp_ref[...]
    outs = _gated_group_norm(y, z_ref[...], gn_ref[...], SSD_N_GROUPS)
    width = d_inner // SSD_N_GROUPS
    for i, o in enumerate(outs):
        g_ref[:, i * width:(i + 1) * width] = o.astype(BF16)

    @pl.when(c == last)
    def _():
        conv_ref[...] = cbuf[0:halo, :]
        for q in range(n_heads // 2):
            sl = slice(q * V7X_LANES, (q + 1) * V7X_LANES)
            ssm_ref[sl, :] = st_t[:, sl].T


def ssd_prompt(z, xbc, dtr, cw, cb, dtb, alog, dexp, gn, consts, *, bsz, seqlen, n_heads):
    m, d_inner = z.shape
    conv_dim = xbc.shape[1]
    L = SSD_CHUNK
    nc = seqlen // L
    tri, e64 = consts
    blk = lambda b, c: (b * nc + c, 0)
    vm = (2 * L * (d_inner + conv_dim + V7X_LANES) * 4 + 2 * L * d_inner * 2
          + (L + 8) * conv_dim * 4 + 2 * L * d_inner * 4
          + 2 * e64.size * 2 + 4 * n_heads * SSD_HEAD_DIM * SSD_D_STATE * 4
          + 8 * L * conv_dim * 4)
    return pl.pallas_call(
        functools.partial(_ssd_prompt_kernel, n_heads=n_heads, d_inner=d_inner),
        grid=(bsz, nc),
        in_specs=[pl.BlockSpec((L, d_inner), blk), pl.BlockSpec((L, conv_dim), blk),
                  pl.BlockSpec((L, V7X_LANES), blk),
                  _const_spec(cw.shape), _const_spec(cb.shape), _const_spec(dtb.shape),
                  _const_spec(alog.shape), _const_spec(dexp.shape), _const_spec(gn.shape),
                  _const_spec(tri.shape), _const_spec(e64.shape)],
        out_specs=[pl.BlockSpec((L, d_inner), blk),
                   pl.BlockSpec((None, n_heads * SSD_HEAD_DIM, SSD_D_STATE), lambda b, c: (b, 0, 0)),
                   pl.BlockSpec((None, V7X_SUBLANES, conv_dim), lambda b, c: (b, 0, 0))],
        out_shape=[jax.ShapeDtypeStruct((m, d_inner), BF16),
                   jax.ShapeDtypeStruct((bsz, n_heads * SSD_HEAD_DIM, SSD_D_STATE), F32),
                   jax.ShapeDtypeStruct((bsz, V7X_SUBLANES, conv_dim), F32)],
        scratch_shapes=[pltpu.VMEM((L + V7X_SUBLANES, conv_dim), F32),
                        pltpu.VMEM((SSD_D_STATE, d_inner), F32),
                        pltpu.VMEM((L, d_inner), F32)],
        compiler_params=_params(("arbitrary", "arbitrary"), vm),
        name="ssd_prompt",
    )(z, xbc, dtr, cw, cb, dtb, alog, dexp, gn, tri, e64)


def _ssd_sample_conv_kernel(xbc_ref, cs_ref, dtr_ref, cw_ref, cb_ref, dtb_ref,
                            x_ref, b_ref, c_ref, dt_ref, ncs_ref, *, d_inner):
    new = xbc_ref[...]
    acc = cb_ref[...] + new * cw_ref[SSD_D_CONV - 1:SSD_D_CONV, :]
    for k in range(SSD_D_CONV - 1):
        acc = acc + cs_ref[k] * cw_ref[k:k + 1, :]
    act = _silu(acc)
    gn_w = SSD_N_GROUPS * SSD_D_STATE
    x_ref[...] = act[:, :d_inner]
    b_ref[...] = act[:, d_inner:d_inner + gn_w]
    c_ref[...] = act[:, d_inner + gn_w:]
    dt_ref[...] = _softplus(dtr_ref[...] + dtb_ref[...])
    for k in range(SSD_D_CONV - 2):
        ncs_ref[k] = cs_ref[k + 1]
    ncs_ref[SSD_D_CONV - 2] = new


def ssd_sample_conv(xbc, cs_t, dtr, cw, cb, dtb, *, d_inner):
    n, conv_dim = xbc.shape
    gn_w = SSD_N_GROUPS * SSD_D_STATE
    args = (xbc, cs_t, dtr, cw, cb, dtb)
    vm = 6 * xbc.size * 4 + 4 * cs_t.size * 4
    return pl.pallas_call(
        functools.partial(_ssd_sample_conv_kernel, d_inner=d_inner),
        grid=(1,),
        in_specs=[_const_spec(a.shape) for a in args],
        out_specs=[_const_spec((n, d_inner)), _const_spec((n, gn_w)), _const_spec((n, gn_w)),
                   _const_spec(dtr.shape), _const_spec(cs_t.shape)],
        out_shape=[jax.ShapeDtypeStruct((n, d_inner), F32), jax.ShapeDtypeStruct((n, gn_w), F32),
                   jax.ShapeDtypeStruct((n, gn_w), F32), jax.ShapeDtypeStruct(dtr.shape, F32),
                   jax.ShapeDtypeStruct(cs_t.shape, F32)],
        compiler_params=_params(("arbitrary",), vm),
        name="ssd_sample_conv",
    )(*args)


def _ssd_sample_state_kernel(st_ref, xt_ref, dtb_ref, dtr_ref, alog_ref, b_ref, c_ref, *rest, sb, n_heads):
    nst_ref, y_ref = rest[-2:]
    heads_per_group = n_heads // SSD_N_GROUPS
    N = SSD_D_STATE
    P = SSD_HEAD_DIM
    a_rep = -jnp.exp(alog_ref[...])
    for s in range(sb):
        dec = jnp.exp(a_rep * dtb_ref[s])
        xdt_t = xt_ref[s] * dtr_ref[s]
        for grp in range(SSD_N_GROUPS):
            b_row = b_ref[s][:, grp * N:(grp + 1) * N]
            c_row = c_ref[s][:, grp * N:(grp + 1) * N]
            news = []
            for r in range(heads_per_group):
                h = grp * heads_per_group + r
                new = st_ref[s, h] * dec[h:h + 1, :] + xdt_t[:, h:h + 1] * b_row
                nst_ref[s, h] = new
                news.append(new)
            hg = jnp.concatenate(news, axis=0).astype(BF16)
            c8 = jnp.broadcast_to(c_row, (V7X_SUBLANES, N)).astype(BF16)
            yg = _dot_nt(c8, hg)
            y_ref[s, :, grp * heads_per_group * P:(grp + 1) * heads_per_group * P] = yg[0:1, :]


def ssd_sample_state(states, layer, stacked, x_t, dt_bcast, dt_row, alog_rep, b_in, c_in, *, sb):
    _, n, n_heads, P, N = states.shape
    d_inner = n_heads * P
    gn_w = b_in.shape[-1]
    lyr = lambda i: (layer, i, 0, 0, 0)
    blk3 = lambda i: (i, 0, 0)
    vm = 4 * sb * n_heads * P * N * 4 + 4 * sb * (P * V7X_LANES + n_heads * V7X_LANES) * 4
    args = [states, x_t, dt_bcast, dt_row, alog_rep, b_in, c_in]
    in_specs = [pl.BlockSpec((None, sb, n_heads, P, N), lyr),
                pl.BlockSpec((sb, P, n_heads), blk3),
                pl.BlockSpec((sb, n_heads, V7X_LANES), blk3),
                pl.BlockSpec((sb, 1, n_heads), blk3),
                _const_spec(alog_rep.shape),
                pl.BlockSpec((sb, 1, gn_w), blk3),
                pl.BlockSpec((sb, 1, gn_w), blk3)]
    aliases = {}
    if stacked is not None:
        aliases = {len(args): 0}
        args.append(stacked)
        in_specs.append(pl.BlockSpec(memory_space=pl.ANY))
    return pl.pallas_call(
        functools.partial(_ssd_sample_state_kernel, sb=sb, n_heads=n_heads),
        grid=(n // sb,),
        in_specs=in_specs,
        out_specs=[pl.BlockSpec((None, sb, n_heads, P, N), lyr),
                   pl.BlockSpec((sb, 1, d_inner), blk3)],
        out_shape=[jax.ShapeDtypeStruct(states.shape, F32),
                   jax.ShapeDtypeStruct((n, 1, d_inner), F32)],
        input_output_aliases=aliases,
        compiler_params=_params(("parallel",), vm),
        name="ssd_sample_state",
    )(*args)


def _ssd_sample_gate_kernel(y_ref, x_ref, z_ref, dexp_ref, gn_ref, g_ref):
    y = y_ref[...] + x_ref[...] * dexp_ref[...]
    outs = _gated_group_norm(y, z_ref[...], gn_ref[...], SSD_N_GROUPS)
    width = y.shape[1] // SSD_N_GROUPS
    for i, o in enumerate(outs):
        g_ref[:, i * width:(i + 1) * width] = o.astype(BF16)


def ssd_sample_gate(y, x, z, dexp, gn):
    args = (y, x, z, dexp, gn)
    return pl.pallas_call(
        _ssd_sample_gate_kernel,
        grid=(1,),
        in_specs=[_const_spec(a.shape) for a in args],
        out_specs=_const_spec(y.shape),
        out_shape=jax.ShapeDtypeStruct(y.shape, BF16),
        compiler_params=_params(("arbitrary",), 8 * y.size * 4),
        name="ssd_sample_gate",
    )(*args)


def _diff_lambda(lam_ref, lam_init):
    v = lam_ref[...]
    d1 = jnp.sum(v[0:1, :] * v[1:2, :], axis=-1, keepdims=True)
    d2 = jnp.sum(v[2:3, :] * v[3:4, :], axis=-1, keepdims=True)
    return jnp.exp(d1) - jnp.exp(d2) + lam_init


V_ROWS_PAD = 16


def _flash_body(lam_ref, q_ref, k_ref, vt_ref, subln_ref, o_ref, scratch, *, t, hb, unroll, lam_init):
    qi = pl.program_id(2)
    m_s, acc_s = scratch[:hb], scratch[hb:]
    q2 = []
    for s in range(hb):
        q = q_ref[:, s * V7X_LANES:(s + 1) * V7X_LANES]
        lane = lax.broadcasted_iota(jnp.int32, q.shape, 1)
        zero = jnp.zeros_like(q)
        q2.append(jnp.concatenate([jnp.where(lane < ATTN_HEAD_DIM, q, zero),
                                   jnp.where(lane >= ATTN_HEAD_DIM, q, zero)], axis=0))
        m_s[s][...] = jnp.full(m_s[s].shape, NEG_BIG, F32)
        acc_s[s][...] = jnp.zeros(acc_s[s].shape, F32)

    def scores(j):
        start = pl.multiple_of(j * t, t)
        return tuple(_dot_nt(k_ref[pl.ds(start, t), s * V7X_LANES:(s + 1) * V7X_LANES], q2[s])
                     for s in range(hb))

    def accumulate(j, scs, masked):
        start = pl.multiple_of(j * t, t)
        for s in range(hb):
            sc = scs[s]
            if masked:
                kv = lax.broadcasted_iota(jnp.int32, sc.shape, 0)
                qq = lax.broadcasted_iota(jnp.int32, sc.shape, 1)
                qq = jnp.where(qq >= t, qq - t, qq)
                sc = jnp.where(kv <= qq, sc, NEG_BIG)
            m_old = m_s[s][...]
            m_new = jnp.maximum(m_old, jnp.max(sc, axis=0, keepdims=True))
            alpha = jnp.exp2(m_old - m_new)
            p = jnp.exp2(sc - m_new).astype(BF16)
            acc_s[s][...] = alpha * acc_s[s][...] + _dot(vt_ref[s, :, pl.ds(start, t)], p)
            m_s[s][...] = m_new

    def run_blocks(j0, n, masked_last):
        scs = scores(j0)
        for u in range(n):
            nxt = scores(j0 + u + 1) if u + 1 < n else None
            accumulate(j0 + u, scs, masked_last and u == n - 1)
            scs = nxt

    def body(g, carry):
        run_blocks(g * unroll, unroll, False)
        return carry

    n_groups = qi // unroll
    lax.fori_loop(0, n_groups, body, 0)
    rem = qi - n_groups * unroll
    for r in range(unroll):
        @pl.when(rem == r)
        def _(r=r):
            run_blocks(n_groups * unroll, r + 1, True)

    lam = _diff_lambda(lam_ref, lam_init)
    gain = subln_ref[...] * (1.0 - lam_init)
    for s in range(hb):
        acc = acc_s[s][...]
        acc = acc[:V7X_LANES, :] * (1.0 / acc[V7X_LANES:V7X_LANES + 1, :])
        o_t = acc[:, :t] - lam * acc[:, t:]
        ms = jnp.mean(o_t * o_t, axis=0, keepdims=True)
        o_t = o_t * lax.rsqrt(ms + RMS_EPS)
        o_ref[:, s * V7X_LANES:(s + 1) * V7X_LANES] = (o_t.T * gain).astype(BF16)


def _decode_body(lam_ref, q_ref, kn_ref, vn_ref, subln_ref, k_refs, v_refs, o_ref, s, *, lam_init):
    n_pages = len(k_refs)
    _, page, n_heads, hd = k_refs[0].shape
    rows = 2 * n_heads
    prow = page * n_heads
    q8 = q_ref[s].astype(F32)
    lane = lax.broadcasted_iota(jnp.int32, q8.shape, 1)
    q16 = jnp.concatenate([jnp.where(lane < ATTN_HEAD_DIM, q8, 0.0),
                           jnp.where(lane >= ATTN_HEAD_DIM, q8, 0.0)], axis=0)
    q16b = q16.astype(BF16)
    r_i = lax.broadcasted_iota(jnp.int32, (rows, prow), 0)
    c_i = lax.broadcasted_iota(jnp.int32, (rows, prow), 1)
    own = (r_i % n_heads) == (c_i % n_heads)

    s_parts = []
    for p in range(n_pages):
        kp = k_refs[p][0].reshape(prow, hd).astype(BF16)
        s_parts.append(jnp.where(own, _dot_nt(q16b, kp), NEG_BIG))
    kn = kn_ref[s]
    kn2 = jnp.concatenate([kn, kn], axis=0)
    s_new = jnp.sum(q16 * kn2, axis=1, keepdims=True)
    m = s_new
    for sp in s_parts:
        m = jnp.maximum(m, jnp.max(sp, axis=1, keepdims=True))
    p_new = jnp.exp2(s_new - m)
    vn = vn_ref[s]
    acc = p_new * jnp.concatenate([vn, vn], axis=0)
    denom = p_new
    for p in range(n_pages):
        pp = jnp.exp2(s_parts[p] - m)
        denom = denom + jnp.sum(pp, axis=1, keepdims=True)
        acc = acc + _dot(pp.astype(BF16), v_refs[p][0].reshape(prow, hd).astype(BF16))

    lam = _diff_lambda(lam_ref, lam_init)
    acc = acc * (1.0 / denom)
    d = acc[:n_heads, :] - lam * acc[n_heads:, :]
    ms = jnp.mean(d * d, axis=1, keepdims=True)
    o_ref[s] = d * lax.rsqrt(ms + RMS_EPS) * (subln_ref[...] * (1.0 - lam_init))


def _attn_kernel(pt_ref, lam_ref, q_ref, k_ref, vt_ref, subln_ref, qd_ref, kn_ref, vn_ref, *rest,
                 n_pages, spp, t, hb, unroll, lam_init):
    del pt_ref
    n_pg = spp * n_pages
    k_refs, v_refs = rest[:n_pg], rest[n_pg:2 * n_pg]
    o_ref, od_ref = rest[2 * n_pg], rest[2 * n_pg + 1]
    scratch = rest[2 * n_pg + 2:]
    for s in range(spp):
        _decode_body(lam_ref, qd_ref, kn_ref, vn_ref, subln_ref,
                     k_refs[s * n_pages:(s + 1) * n_pages], v_refs[s * n_pages:(s + 1) * n_pages],
                     od_ref, s, lam_init=lam_init)
    _flash_body(lam_ref, q_ref, k_ref, vt_ref, subln_ref, o_ref, scratch,
                t=t, hb=hb, unroll=unroll, lam_init=lam_init)


def diff_attn(page_table, lam_vecs, q, k, v_t, subln, q_dec, k_new, v_new, cache_k, cache_v, *,
              bsz, seqlen, n_heads, lam_init, t=256, hb=4, unroll=2):
    nq = seqlen // t
    n_hg = n_heads // hb
    n_steps = bsz * n_hg * nq
    n, n_pages = page_table.shape
    spp, ragged = divmod(n, n_steps)
    assert spp >= 1 and ragged == 0, "sample sequences must tile the prompt attention grid"
    _, page, _, hd = cache_k.shape
    vr = v_t.shape[2]
    w = hb * V7X_LANES
    step = lambda b, h, i: (b * n_hg + h) * nq + i
    qblk = lambda b, h, i, pt: (b * nq + i, h)
    dec = lambda b, h, i, pt: (step(b, h, i), 0, 0)
    const2 = lambda b, h, i, pt: (0, 0)
    page_specs = [pl.BlockSpec((1, page, n_heads, hd),
                               functools.partial(lambda b, h, i, pt, s, p: (pt[step(b, h, i) * spp + s, p], 0, 0, 0),
                                                 s=s, p=p))
                  for s in range(spp) for p in range(n_pages)]
    resident = dict(pipeline_mode=pl.Buffered(1))
    vm = (2 * 2 * t * w * 2 + seqlen * w * 2 + hb * vr * seqlen * 2 + hb * (vr + 8) * 2 * t * 4
          + hb * 4 * t * 2 * t * 4
          + 2 * 2 * spp * n_pages * page * n_heads * hd * 4 + 4 * 2 * n_heads * page * n_heads * n_pages * 4)
    grid_spec = pltpu.PrefetchScalarGridSpec(
        num_scalar_prefetch=1,
        grid=(bsz, n_hg, nq),
        in_specs=[pl.BlockSpec(lam_vecs.shape, const2),
                  pl.BlockSpec((t, w), qblk),
                  pl.BlockSpec((seqlen, w), lambda b, h, i, pt: (b, h), **resident),
                  pl.BlockSpec((None, hb, vr, seqlen), lambda b, h, i, pt: (b, h, 0, 0), **resident),
                  pl.BlockSpec(subln.shape, const2),
                  pl.BlockSpec((spp, n_heads, hd), dec), pl.BlockSpec((spp, n_heads, hd), dec),
                  pl.BlockSpec((spp, n_heads, hd), dec)]
                 + page_specs + page_specs,
        out_specs=[pl.BlockSpec((t, w), qblk), pl.BlockSpec((spp, n_heads, hd), dec)],
        scratch_shapes=[pltpu.VMEM((1, 2 * t), F32)] * hb + [pltpu.VMEM((vr, 2 * t), F32)] * hb,
    )
    return pl.pallas_call(
        functools.partial(_attn_kernel, n_pages=n_pages, spp=spp, t=t, hb=hb, unroll=unroll, lam_init=lam_init),
        grid_spec=grid_spec,
        out_shape=[jax.ShapeDtypeStruct(q.shape, BF16), jax.ShapeDtypeStruct((n, n_heads, hd), F32)],
        compiler_params=_params(("parallel", "parallel", "arbitrary"), vm),
        name="diff_attn",
    )(page_table, lam_vecs, q, k, v_t, subln, q_dec, k_new, v_new,
      *([cache_k] * (spp * n_pages)), *([cache_v] * (spp * n_pages)))


def _rope_tables(pos):
    half = ROT_DIM // 2
    inv = ROPE_THETA ** (-jnp.arange(0, ROT_DIM, 2, dtype=F32) / ROT_DIM)
    ang = pos.astype(F32)[:, None] * inv[None, :]
    cos, sin = jnp.cos(ang), jnp.sin(ang)
    n = pos.shape[0]
    pad = jnp.zeros((n, ATTN_HEAD_DIM - ROT_DIM), F32)
    zeros = jnp.zeros((n, half), F32)
    c_map = jnp.concatenate([cos, cos, pad + 1.0], axis=1)
    sa_map = jnp.concatenate([-sin, zeros, pad], axis=1)
    sb_map = jnp.concatenate([zeros, sin, pad], axis=1)
    return tuple(jnp.concatenate([t, t], axis=1) for t in (c_map, sa_map, sb_map))


def _row(v, width=None):
    v = v.astype(F32).reshape(1, -1)
    if width is not None and v.shape[1] < width:
        v = jnp.pad(v, ((0, 0), (0, width - v.shape[1])))
    return v


def kernel(x_prompt, x_sample, state_ssm, state_conv, cache_k, cache_v, page_table, ssd_norm_pre, ssd_norm_post, ssd_w_in, ssd_conv_w, ssd_conv_b, ssd_dt_bias, ssd_a_log, ssd_d, ssd_gate_norm, ssd_w_out, mlp_norm_pre, mlp_norm_post, mlp_w_up, mlp_w_down, kv_norm, w_kv, attn_norm_pre, attn_norm_post, attn_w_q, attn_lambda_q1, attn_lambda_k1, attn_lambda_q2, attn_lambda_k2, attn_subln, attn_w_o):
    bsz, seqlen, d_model = x_prompt.shape
    n_dec = x_sample.shape[0]
    n_ssd = ssd_w_in.shape[0]
    depth = mlp_w_up.shape[0]
    n_heads_ssd = ssd_a_log.shape[1]
    d_inner = n_heads_ssd * SSD_HEAD_DIM
    conv_dim = ssd_conv_w.shape[2]
    qk_dim = attn_w_q.shape[2]
    n_heads = qk_dim // V7X_LANES
    m_p = bsz * seqlen

    hp = x_prompt.reshape(m_p, d_model)
    hs = x_sample.reshape(n_dec, d_model)

    tm_p = 256
    tm_mlp = 512
    tm_s = n_dec

    L = SSD_CHUNK
    tri = (jnp.arange(L)[:, None] >= jnp.arange(L)[None, :]).astype(F32)
    hrow = jnp.arange(V7X_LANES)[:, None]
    e64 = (hrow == (jnp.arange(d_inner)[None, :] // SSD_HEAD_DIM)).astype(BF16)

    tab_p = _rope_tables(jnp.arange(seqlen))
    past_len = page_table.shape[1] * cache_k.shape[1]
    tab_s = tuple(jnp.broadcast_to(t, (n_dec, V7X_LANES)) for t in _rope_tables(jnp.full((1,), past_len)))

    ssm_p, conv_p, conv_s = [], [], []
    ssm_s = None
    k_p = v_p = k_s = v_s = None
    kb_p = vt_p = None

    for i in range(depth):
        if i < n_ssd:
            j = i
            w_in = ssd_w_in[j]
            w_in_b = jnp.concatenate(
                [w_in[:, :d_inner + conv_dim],
                 jnp.pad(w_in[:, d_inner + conv_dim:], ((0, 0), (0, V7X_LANES - n_heads_ssd)))], axis=1).astype(BF16)
            w_out_b = ssd_w_out[j].astype(BF16)
            g_pre = _row(ssd_norm_pre[j])
            g_post = _row(ssd_norm_post[j])
            cw = ssd_conv_w[j].astype(F32)
            cb = _row(ssd_conv_b[j])
            dtb = _row(ssd_dt_bias[j], V7X_LANES)
            alog = _row(ssd_a_log[j], V7X_LANES)
            dexp = jnp.repeat(ssd_d[j].astype(F32), SSD_HEAD_DIM).reshape(1, d_inner)
            gn = _row(ssd_gate_norm[j])

            z, xbc, dtr = ssd_in_proj(hp, g_pre, w_in_b, d_inner=d_inner, conv_dim=conv_dim, tm=tm_p)
            g, sp, cp = ssd_prompt(z, xbc, dtr, cw, cb, dtb, alog, dexp, gn, (tri, e64),
                                   bsz=bsz, seqlen=seqlen, n_heads=n_heads_ssd)
            hp = out_proj_residual(g, w_out_b, g_post, hp, tm=tm_p)
            ssm_p.append(sp.reshape(bsz, n_heads_ssd, SSD_HEAD_DIM, SSD_D_STATE))
            conv_p.append(cp[:, V7X_SUBLANES - (SSD_D_CONV - 1):, :])

            zs, xbcs, dtrs = ssd_in_proj(hs, g_pre, w_in_b, d_inner=d_inner, conv_dim=conv_dim, tm=tm_s)
            cs_t = jnp.transpose(state_conv[j].astype(F32), (1, 0, 2))
            xs, bs_, cs_, dts, ncs = ssd_sample_conv(xbcs, cs_t, dtrs, cw, cb, dtb, d_inner=d_inner)
            x_t = jnp.transpose(xs.reshape(n_dec, n_heads_ssd, SSD_HEAD_DIM), (0, 2, 1))
            dt_h = dts[:, :n_heads_ssd]
            dt_bcast = jnp.broadcast_to(dt_h[:, :, None], (n_dec, n_heads_ssd, V7X_LANES))
            alog_rep = jnp.broadcast_to(ssd_a_log[j].astype(F32)[:, None], (n_heads_ssd, V7X_LANES))
            ssm_s, ys = ssd_sample_state(state_ssm, j, ssm_s, x_t, dt_bcast, dt_h[:, None, :], alog_rep,
                                         bs_[:, None, :], cs_[:, None, :], sb=4)
            gs = ssd_sample_gate(ys.reshape(n_dec, d_inner), xs, zs, dexp, gn)
            hs = out_proj_residual(gs, w_out_b, g_post, hs, tm=tm_s)
            conv_s.append(jnp.transpose(ncs, (1, 0, 2)).astype(state_conv.dtype))
        else:
            j = i - n_ssd
            if j == 0:
                w_kv_b = w_kv.astype(BF16)
                g_kv = _row(kv_norm)
                k_p, v_p, kb_p, vb_p = shared_kv_proj(hp, g_kv, w_kv_b, tab_p, tm=tm_p, n_pos_blocks=seqlen // tm_p)
                k_s, v_s, _, _ = shared_kv_proj(hs, g_kv, w_kv_b, tab_s, tm=tm_s, n_pos_blocks=1)
                vt_p = jnp.transpose(vb_p.reshape(bsz, seqlen, n_heads, V7X_LANES), (0, 2, 3, 1))
                ones_rows = jnp.zeros((bsz, n_heads, V_ROWS_PAD, seqlen), BF16).at[:, :, 0, :].set(1.0)
                vt_p = jnp.concatenate([vt_p, ones_rows], axis=2)
            lam_init = 0.8 - 0.6 * math.exp(-0.3 * i)
            lam_vecs = jnp.pad(
                jnp.stack([attn_lambda_q1[j], attn_lambda_k1[j], attn_lambda_q2[j], attn_lambda_k2[j]]).astype(F32),
                ((0, V7X_SUBLANES - 4), (0, V7X_LANES - ATTN_HEAD_DIM)))
            g_pre = _row(attn_norm_pre[j])
            g_post = _row(attn_norm_post[j])
            w_q_b = attn_w_q[j].astype(BF16)
            w_o_b = attn_w_o[j].astype(BF16)
            subln = _row(attn_subln[j])

            qp = attn_q_proj(hp, g_pre, w_q_b, tab_p, tm=tm_p, n_pos_blocks=seqlen // tm_p)
            qs = attn_q_proj(hs, g_pre, w_q_b, tab_s, tm=tm_s, n_pos_blocks=1)
            heads = lambda a: a.reshape(n_dec, n_heads, V7X_LANES)
            op, os_ = diff_attn(page_table, lam_vecs, qp, kb_p, vt_p, subln, heads(qs), heads(k_s), heads(v_s),
                                cache_k, cache_v, bsz=bsz, seqlen=seqlen, n_heads=n_heads, lam_init=lam_init)
            hp = out_proj_residual(op, w_o_b, g_post, hp, tm=tm_p)
            hs = out_proj_residual(os_.reshape(n_dec, qk_dim), w_o_b, g_post, hs, tm=tm_s)

        g1 = _row(mlp_norm_pre[i])
        g2 = _row(mlp_norm_post[i])
        wu = mlp_w_up[i].astype(BF16)
        wd = mlp_w_down[i].astype(BF16)
        hp = mlp_residual(hp, g1, wu, wd, g2, tm=tm_mlp)
        hs = mlp_residual(hs, g1, wu, wd, g2, tm=tm_s)

    y_prompt = hp.reshape(bsz, seqlen, d_model)
    y_sample = hs.reshape(n_dec, 1, d_model)
    p_k = k_p.reshape(bsz, seqlen, n_heads, V7X_LANES)
    p_v = v_p.reshape(bsz, seqlen, n_heads, V7X_LANES)
    s_k = k_s.reshape(n_dec, 1, n_heads, V7X_LANES)
    s_v = v_s.reshape(n_dec, 1, n_heads, V7X_LANES)
    return (y_prompt, y_sample, jnp.stack(ssm_p), jnp.stack(conv_p), p_k, p_v,
            ssm_s.astype(state_ssm.dtype), jnp.stack(conv_s), s_k, s_v)
```

```python
import functools
import math

import jax
import jax.numpy as jnp
from jax import lax
from jax.experimental import pallas as pl
from jax.experimental.pallas import tpu as pltpu

F32 = jnp.float32
BF16 = jnp.bfloat16

RMS_EPS = 1e-6
ROPE_THETA = 500000.0

V7X_LANES = 128
V7X_SUBLANES = 8
V7X_VMEM_BYTES = 64 * 1024 * 1024

SSD_HEAD_DIM = 64
SSD_N_GROUPS = 4
SSD_D_STATE = 128
SSD_D_CONV = 4
SSD_CHUNK = 128
ATTN_HEAD_DIM = 64
ROT_DIM = ATTN_HEAD_DIM // 4

NEG_BIG = -1e30


def _vmem_limit(nbytes):
    return int(min(V7X_VMEM_BYTES * 7 // 8, max(32 * 1024 * 1024, nbytes * 3 // 2)))


def _params(sem, vmem_bytes):
    return pltpu.CompilerParams(dimension_semantics=sem, vmem_limit_bytes=_vmem_limit(vmem_bytes))


def _const_spec(shape):
    nd = len(shape)
    return pl.BlockSpec(shape, lambda *_: (0,) * nd, pipeline_mode=pl.Buffered(1))


def _whole_spec(shape):
    nd = len(shape)
    return pl.BlockSpec(shape, lambda *_: (0,) * nd)


def _rms(x, g):
    ms = jnp.mean(x * x, axis=-1, keepdims=True)
    return x * lax.rsqrt(ms + RMS_EPS) * g


def _silu(x):
    return x * (1.0 / (1.0 + jnp.exp(-x)))


def _softplus(x):
    return jnp.maximum(x, 0.0) + jnp.log1p(jnp.exp(-jnp.abs(x)))


def _dot(a, b):
    return jnp.dot(a, b, preferred_element_type=F32)


def _dot_nt(a, b):
    return lax.dot_general(a, b, (((1,), (1,)), ((), ())), preferred_element_type=F32)


def _rope(t, cos, sa, sb):
    up = pltpu.roll(t, V7X_LANES - ROT_DIM // 2, 1)
    dn = pltpu.roll(t, ROT_DIM // 2, 1)
    return t * cos + up * sa + dn * sb


def _inproj_kernel(x_ref, g_ref, w_ref, z_ref, xbc_ref, dt_ref, *, d_inner, conv_dim):
    xn = _rms(x_ref[...], g_ref[...]).astype(BF16)
    z_ref[...] = _dot(xn, w_ref[:, 0:d_inner])
    xbc_ref[...] = _dot(xn, w_ref[:, d_inner:d_inner + conv_dim])
    dt_ref[...] = _dot(xn, w_ref[:, d_inner + conv_dim:])


def ssd_in_proj(x, g, w, *, d_inner, conv_dim, tm):
    m, d = x.shape
    n = w.shape[1]
    dtw = n - d_inner - conv_dim
    vm = 2 * tm * d * 4 + d * n * 2 + 2 * tm * n * 4
    return pl.pallas_call(
        functools.partial(_inproj_kernel, d_inner=d_inner, conv_dim=conv_dim),
        grid=(m // tm,),
        in_specs=[pl.BlockSpec((tm, d), lambda i: (i, 0)), _const_spec((1, d)), _const_spec((d, n))],
        out_specs=[pl.BlockSpec((tm, d_inner), lambda i: (i, 0)),
                   pl.BlockSpec((tm, conv_dim), lambda i: (i, 0)),
                   pl.BlockSpec((tm, dtw), lambda i: (i, 0))],
        out_shape=[jax.ShapeDtypeStruct((m, d_inner), F32),
                   jax.ShapeDtypeStruct((m, conv_dim), F32),
                   jax.ShapeDtypeStruct((m, dtw), F32)],
        compiler_params=_params(("parallel",), vm),
        name="ssd_in_proj",
    )(x, g, w)


def _qproj_kernel(x_ref, g_ref, w_ref, cos_ref, sa_ref, sb_ref, q_ref, *, n_heads, scale):
    xn = _rms(x_ref[...], g_ref[...]).astype(BF16)
    cos, sa, sb = cos_ref[...], sa_ref[...], sb_ref[...]
    t = _dot(xn, w_ref[...])
    for h in range(n_heads):
        sl = slice(h * V7X_LANES, (h + 1) * V7X_LANES)
        q_ref[:, sl] = (_rope(t[:, sl], cos, sa, sb) * scale).astype(BF16)


def attn_q_proj(x, g, w, tables, *, tm, n_pos_blocks):
    m, d = x.shape
    n = w.shape[1]
    n_heads = n // V7X_LANES
    tab_spec = pl.BlockSpec((tm, V7X_LANES), lambda i: (i % n_pos_blocks, 0))
    vm = 2 * tm * d * 4 + d * n * 2 + 2 * tm * n * 2 + tm * n * 4 + 6 * tm * V7X_LANES * 4
    return pl.pallas_call(
        functools.partial(_qproj_kernel, n_heads=n_heads, scale=ATTN_HEAD_DIM ** -0.5 * math.log2(math.e)),
        grid=(m // tm,),
        in_specs=[pl.BlockSpec((tm, d), lambda i: (i, 0)), _const_spec((1, d)), _const_spec((d, n)),
                  tab_spec, tab_spec, tab_spec],
        out_specs=pl.BlockSpec((tm, n), lambda i: (i, 0)),
        out_shape=jax.ShapeDtypeStruct((m, n), BF16),
        compiler_params=_params(("parallel",), vm),
        name="attn_q_proj",
    )(x, g, w, *tables)


def _kvproj_kernel(x_ref, g_ref, w_ref, cos_ref, sa_ref, sb_ref, k_ref, v_ref, kb_ref, vb_ref, *, n_heads):
    xn = _rms(x_ref[...], g_ref[...]).astype(BF16)
    cos, sa, sb = cos_ref[...], sa_ref[...], sb_ref[...]
    qk = n_heads * V7X_LANES
    t = _dot(xn, w_ref[:, :qk])
    for h in range(n_heads):
        sl = slice(h * V7X_LANES, (h + 1) * V7X_LANES)
        k = _rope(t[:, sl], cos, sa, sb)
        k_ref[:, sl] = k
        kb_ref[:, sl] = k.astype(BF16)
    v = _dot(xn, w_ref[:, qk:])
    v_ref[...] = v
    vb_ref[...] = v.astype(BF16)


def shared_kv_proj(x, g, w, tables, *, tm, n_pos_blocks):
    m, d = x.shape
    n = w.shape[1]
    qk = n // 2
    n_heads = qk // V7X_LANES
    tab_spec = pl.BlockSpec((tm, V7X_LANES), lambda i: (i % n_pos_blocks, 0))
    row = lambda i: (i, 0)
    vm = 2 * tm * d * 4 + d * n * 2 + 2 * tm * n * 6 + tm * n * 4 + 6 * tm * V7X_LANES * 4
    return pl.pallas_call(
        functools.partial(_kvproj_kernel, n_heads=n_heads),
        grid=(m // tm,),
        in_specs=[pl.BlockSpec((tm, d), row), _const_spec((1, d)), _const_spec((d, n)),
                  tab_spec, tab_spec, tab_spec],
        out_specs=[pl.BlockSpec((tm, qk), row)] * 4,
        out_shape=[jax.ShapeDtypeStruct((m, qk), F32), jax.ShapeDtypeStruct((m, qk), F32),
                   jax.ShapeDtypeStruct((m, qk), BF16), jax.ShapeDtypeStruct((m, qk), BF16)],
        compiler_params=_params(("parallel",), vm),
        name="shared_kv_proj",
    )(x, g, w, *tables)


def _outproj_kernel(a_ref, w_ref, g_ref, h_ref, o_ref):
    y = _dot(a_ref[...].astype(BF16), w_ref[...])
    o_ref[...] = h_ref[...] + _rms(y, g_ref[...])


def out_proj_residual(a, w, g_post, h, *, tm):
    m, k = a.shape
    d = w.shape[1]
    row = lambda i: (i, 0)
    vm = 2 * tm * k * a.dtype.itemsize + k * d * 2 + 5 * tm * d * 4
    return pl.pallas_call(
        _outproj_kernel,
        grid=(m // tm,),
        in_specs=[pl.BlockSpec((tm, k), row), _const_spec((k, d)), _const_spec((1, d)),
                  pl.BlockSpec((tm, d), row)],
        out_specs=pl.BlockSpec((tm, d), row),
        out_shape=jax.ShapeDtypeStruct((m, d), F32),
        compiler_params=_params(("parallel",), vm),
        name="out_proj_residual",
    )(a, w, g_post, h)


def _mlp_kernel(h_ref, g1_ref, wu_ref, wd_ref, g2_ref, o_ref, *, tf):
    h = h_ref[...]
    xn = _rms(h, g1_ref[...]).astype(BF16)
    d_ff = wu_ref.shape[1]
    acc = jnp.zeros(h.shape, F32)
    for f in range(d_ff // tf):
        u = jnp.maximum(_dot(xn, wu_ref[:, f * tf:(f + 1) * tf]), 0.0)
        acc = acc + _dot((u * u).astype(BF16), wd_ref[f * tf:(f + 1) * tf, :])
    o_ref[...] = h + _rms(acc, g2_ref[...])


def mlp_residual(h, g_pre, w_up, w_down, g_post, *, tm, tf=512):
    m, d = h.shape
    d_ff = w_up.shape[1]
    row = lambda i: (i, 0)
    vm = 4 * tm * d * 4 + 2 * d * d_ff * 2 + tm * d * 4 + 2 * tm * tf * 4
    return pl.pallas_call(
        functools.partial(_mlp_kernel, tf=tf),
        grid=(m // tm,),
        in_specs=[pl.BlockSpec((tm, d), row), _const_spec((1, d)), _const_spec((d, d_ff)),
                  _const_spec((d_ff, d)), _const_spec((1, d))],
        out_specs=pl.BlockSpec((tm, d), row),
        out_shape=jax.ShapeDtypeStruct((m, d), F32),
        compiler_params=_params(("parallel",), vm),
        name="mlp_residual",
    )(h, g_pre, w_up, w_down, g_post)


def _expand3(v, e):
    hi = v.astype(BF16)
    r1 = v - hi.astype(F32)
    mid = r1.astype(BF16)
    lo = (r1 - mid.astype(F32)).astype(BF16)
    return _dot(hi, e) + _dot(mid, e) + _dot(lo, e)


def _gated_group_norm(y, z, gate_norm, n_groups):
    g = y * _silu(z)
    width = g.shape[1] // n_groups
    outs = []
    for i in range(n_groups):
        gg = g[:, i * width:(i + 1) * width]
        ms = jnp.mean(gg * gg, axis=-1, keepdims=True)
        outs.append(gg * lax.rsqrt(ms + RMS_EPS) * gate_norm[:, i * width:(i + 1) * width])
    return outs


def _ssd_prompt_kernel(z_ref, xbc_ref, dtr_ref, cw_ref, cb_ref, dtb_ref, alog_ref, dexp_ref, gn_ref,
                       tri_ref, e64_ref,
                       g_ref, ssm_ref, conv_ref,
                       cbuf, st_t, y_s, *, n_heads, d_inner):
    L = SSD_CHUNK
    N = SSD_D_STATE
    gn_w = SSD_N_GROUPS * N
    heads_per_group = n_heads // SSD_N_GROUPS
    c = pl.program_id(1)
    last = pl.num_programs(1) - 1
    halo = V7X_SUBLANES

    @pl.when(c == 0)
    def _():
        cbuf[0:halo, :] = jnp.zeros((halo, cbuf.shape[1]), F32)
        st_t[...] = jnp.zeros(st_t.shape, F32)

    cbuf[halo:halo + L, :] = xbc_ref[...]
    xw = cbuf[...]
    acc = cb_ref[...] + xw[halo:, :] * cw_ref[SSD_D_CONV - 1:SSD_D_CONV, :]
    for k in range(1, SSD_D_CONV):
        acc = acc + pltpu.roll(xw, k, 0)[halo:, :] * cw_ref[SSD_D_CONV - 1 - k:SSD_D_CONV - k, :]
    act = _silu(acc)
    x = act[:, :d_inner]
    cbuf[0:halo, :] = xw[L:, :]

    dt = _softplus(dtr_ref[...] + dtb_ref[...])
    a = -jnp.exp(alog_ref[...])
    acum = jnp.dot(tri_ref[...], a * dt, precision=lax.Precision.HIGHEST, preferred_element_type=F32)
    acum_last = acum[L - 1:L, :]
    acum_t = acum.T
    cd = _expand3(jnp.broadcast_to(jnp.exp(acum_last), (V7X_SUBLANES, V7X_LANES)), e64_ref[...])[0:1, :]
    e64 = e64_ref[...]
    dt_x = _dot(dt.astype(BF16), e64)
    w_x = _dot((jnp.exp(acum_last - acum) * dt).astype(BF16), e64)
    e1_x = _dot(jnp.exp(acum).astype(BF16), e64)
    xdt = (x * dt_x).astype(BF16)
    xs = (x * w_x).astype(BF16)

    li = lax.broadcasted_iota(jnp.int32, (L, L), 0)
    si = lax.broadcasted_iota(jnp.int32, (L, L), 1)
    causal = li >= si
    lo_half = lax.broadcasted_iota(jnp.int32, (L, V7X_LANES), 1) < SSD_HEAD_DIM

    for grp in range(SSD_N_GROUPS):
        b_g = act[:, d_inner + grp * N:d_inner + (grp + 1) * N]
        c_g = act[:, d_inner + gn_w + grp * N:d_inner + gn_w + (grp + 1) * N].astype(BF16)
        cb_g = jnp.where(causal, _dot_nt(c_g, b_g.astype(BF16)), 0.0)
        bt_g = b_g.T.astype(BF16)
        for pr in range(heads_per_group // 2):
            q = grp * (heads_per_group // 2) + pr
            sl = slice(q * V7X_LANES, (q + 1) * V7X_LANES)
            st_prev = st_t[:, sl]
            y_off = _dot(c_g, st_prev.astype(BF16)) * e1_x[:, sl]
            st_t[:, sl] = st_prev * cd[:, sl] + _dot(bt_g, xs[:, sl])
            xp = xdt[:, sl]
            ys = []
            for h in (2 * q, 2 * q + 1):
                seg = jnp.minimum(jnp.broadcast_to(acum[:, h:h + 1], (L, L)) - acum_t[h:h + 1, :], 0.0)
                ys.append(_dot((jnp.exp(seg) * cb_g).astype(BF16), xp))
            y_s[:, sl] = jnp.where(lo_half, ys[0], ys[1]) + y_off

    y = y_s[...] + x * dexp_ref[...]
    outs = _gated_group_norm(y, z_ref[...], gn_ref[...], SSD_N_GROUPS)
    width = d_inner // SSD_N_GROUPS
    for i, o in enumerate(outs):
        g_ref[:, i * width:(i + 1) * width] = o.astype(BF16)

    @pl.when(c == last)
    def _():
        conv_ref[...] = cbuf[0:halo, :]
        for q in range(n_heads // 2):
            sl = slice(q * V7X_LANES, (q + 1) * V7X_LANES)
            ssm_ref[sl, :] = st_t[:, sl].T


def ssd_prompt(z, xbc, dtr, cw, cb, dtb, alog, dexp, gn, consts, *, bsz, seqlen, n_heads):
    m, d_inner = z.shape
    conv_dim = xbc.shape[1]
    L = SSD_CHUNK
    nc = seqlen // L
    tri, e64 = consts
    blk = lambda b, c: (b * nc + c, 0)
    vm = (2 * L * (d_inner + conv_dim + V7X_LANES) * 4 + 2 * L * d_inner * 2
          + (L + 8) * conv_dim * 4 + 2 * L * d_inner * 4
          + 2 * e64.size * 2 + 4 * n_heads * SSD_HEAD_DIM * SSD_D_STATE * 4
          + 8 * L * conv_dim * 4)
    return pl.pallas_call(
        functools.partial(_ssd_prompt_kernel, n_heads=n_heads, d_inner=d_inner),
        grid=(bsz, nc),
        in_specs=[pl.BlockSpec((L, d_inner), blk), pl.BlockSpec((L, conv_dim), blk),
                  pl.BlockSpec((L, V7X_LANES), blk),
                  _const_spec(cw.shape), _const_spec(cb.shape), _const_spec(dtb.shape),
                  _const_spec(alog.shape), _const_spec(dexp.shape), _const_spec(gn.shape),
                  _const_spec(tri.shape), _const_spec(e64.shape)],
        out_specs=[pl.BlockSpec((L, d_inner), blk),
                   pl.BlockSpec((None, n_heads * SSD_HEAD_DIM, SSD_D_STATE), lambda b, c: (b, 0, 0)),
                   pl.BlockSpec((None, V7X_SUBLANES, conv_dim), lambda b, c: (b, 0, 0))],
        out_shape=[jax.ShapeDtypeStruct((m, d_inner), BF16),
                   jax.ShapeDtypeStruct((bsz, n_heads * SSD_HEAD_DIM, SSD_D_STATE), F32),
                   jax.ShapeDtypeStruct((bsz, V7X_SUBLANES, conv_dim), F32)],
        scratch_shapes=[pltpu.VMEM((L + V7X_SUBLANES, conv_dim), F32),
                        pltpu.VMEM((SSD_D_STATE, d_inner), F32),
                        pltpu.VMEM((L, d_inner), F32)],
        compiler_params=_params(("arbitrary", "arbitrary"), vm),
        name="ssd_prompt",
    )(z, xbc, dtr, cw, cb, dtb, alog, dexp, gn, tri, e64)


def _ssd_sample_conv_kernel(xbc_ref, cs_ref, dtr_ref, cw_ref, cb_ref, dtb_ref,
                            x_ref, b_ref, c_ref, dt_ref, ncs_ref, *, d_inner):
    new = xbc_ref[...]
    acc = cb_ref[...] + new * cw_ref[SSD_D_CONV - 1:SSD_D_CONV, :]
    for k in range(SSD_D_CONV - 1):
        acc = acc + cs_ref[k] * cw_ref[k:k + 1, :]
    act = _silu(acc)
    gn_w = SSD_N_GROUPS * SSD_D_STATE
    x_ref[...] = act[:, :d_inner]
    b_ref[...] = act[:, d_inner:d_inner + gn_w]
    c_ref[...] = act[:, d_inner + gn_w:]
    dt_ref[...] = _softplus(dtr_ref[...] + dtb_ref[...])
    for k in range(SSD_D_CONV - 2):
        ncs_ref[k] = cs_ref[k + 1]
    ncs_ref[SSD_D_CONV - 2] = new


def ssd_sample_conv(xbc, cs_t, dtr, cw, cb, dtb, *, d_inner):
    n, conv_dim = xbc.shape
    gn_w = SSD_N_GROUPS * SSD_D_STATE
    args = (xbc, cs_t, dtr, cw, cb, dtb)
    vm = 6 * xbc.size * 4 + 4 * cs_t.size * 4
    return pl.pallas_call(
        functools.partial(_ssd_sample_conv_kernel, d_inner=d_inner),
        grid=(1,),
        in_specs=[_whole_spec(a.shape) for a in args],
        out_specs=[_whole_spec((n, d_inner)), _whole_spec((n, gn_w)), _whole_spec((n, gn_w)),
                   _whole_spec(dtr.shape), _whole_spec(cs_t.shape)],
        out_shape=[jax.ShapeDtypeStruct((n, d_inner), F32), jax.ShapeDtypeStruct((n, gn_w), F32),
                   jax.ShapeDtypeStruct((n, gn_w), F32), jax.ShapeDtypeStruct(dtr.shape, F32),
                   jax.ShapeDtypeStruct(cs_t.shape, F32)],
        compiler_params=_params(("arbitrary",), vm),
        name="ssd_sample_conv",
    )(*args)


def _ssd_sample_state_kernel(st_ref, xt_ref, dtb_ref, dtr_ref, alog_ref, b_ref, c_ref, *rest, sb, n_heads):
    nst_ref, y_ref = rest[-2:]
    heads_per_group = n_heads // SSD_N_GROUPS
    N = SSD_D_STATE
    P = SSD_HEAD_DIM
    a_rep = -jnp.exp(alog_ref[...])
    for s in range(sb):
        dec = jnp.exp(a_rep * dtb_ref[s])
        xdt_t = xt_ref[s] * dtr_ref[s]
        for grp in range(SSD_N_GROUPS):
            b_row = b_ref[s][:, grp * N:(grp + 1) * N]
            c_row = c_ref[s][:, grp * N:(grp + 1) * N]
            news = []
            for r in range(heads_per_group):
                h = grp * heads_per_group + r
                new = st_ref[s, h] * dec[h:h + 1, :] + xdt_t[:, h:h + 1] * b_row
                nst_ref[s, h] = new
                news.append(new)
            hg = jnp.concatenate(news, axis=0).astype(BF16)
            c8 = jnp.broadcast_to(c_row, (V7X_SUBLANES, N)).astype(BF16)
            yg = _dot_nt(c8, hg)
            y_ref[s, :, grp * heads_per_group * P:(grp + 1) * heads_per_group * P] = yg[0:1, :]


def ssd_sample_state(states, layer, stacked, x_t, dt_bcast, dt_row, alog_rep, b_in, c_in, *, sb):
    _, n, n_heads, P, N = states.shape
    d_inner = n_heads * P
    gn_w = b_in.shape[-1]
    lyr = lambda i: (layer, i, 0, 0, 0)
    blk3 = lambda i: (i, 0, 0)
    vm = 4 * sb * n_heads * P * N * 4 + 4 * sb * (P * V7X_LANES + n_heads * V7X_LANES) * 4
    args = [states, x_t, dt_bcast, dt_row, alog_rep, b_in, c_in]
    in_specs = [pl.BlockSpec((None, sb, n_heads, P, N), lyr),
                pl.BlockSpec((sb, P, n_heads), blk3),
                pl.BlockSpec((sb, n_heads, V7X_LANES), blk3),
                pl.BlockSpec((sb, 1, n_heads), blk3),
                _const_spec(alog_rep.shape),
                pl.BlockSpec((sb, 1, gn_w), blk3),
                pl.BlockSpec((sb, 1, gn_w), blk3)]
    aliases = {}
    if stacked is not None:
        aliases = {len(args): 0}
        args.append(stacked)
        in_specs.append(pl.BlockSpec(memory_space=pl.ANY))
    return pl.pallas_call(
        functools.partial(_ssd_sample_state_kernel, sb=sb, n_heads=n_heads),
        grid=(n // sb,),
        in_specs=in_specs,
        out_specs=[pl.BlockSpec((None, sb, n_heads, P, N), lyr),
                   pl.BlockSpec((sb, 1, d_inner), blk3)],
        out_shape=[jax.ShapeDtypeStruct(states.shape, F32),
                   jax.ShapeDtypeStruct((n, 1, d_inner), F32)],
        input_output_aliases=aliases,
        compiler_params=_params(("parallel",), vm),
        name="ssd_sample_state",
    )(*args)


def _ssd_sample_gate_kernel(y_ref, x_ref, z_ref, dexp_ref, gn_ref, g_ref):
    y = y_ref[...] + x_ref[...] * dexp_ref[...]
    outs = _gated_group_norm(y, z_ref[...], gn_ref[...], SSD_N_GROUPS)
    width = y.shape[1] // SSD_N_GROUPS
    for i, o in enumerate(outs):
        g_ref[:, i * width:(i + 1) * width] = o.astype(BF16)


def ssd_sample_gate(y, x, z, dexp, gn):
    args = (y, x, z, dexp, gn)
    return pl.pallas_call(
        _ssd_sample_gate_kernel,
        grid=(1,),
        in_specs=[_whole_spec(a.shape) for a in args],
        out_specs=_whole_spec(y.shape),
        out_shape=jax.ShapeDtypeStruct(y.shape, BF16),
        compiler_params=_params(("arbitrary",), 8 * y.size * 4),
        name="ssd_sample_gate",
    )(*args)


def _diff_lambda(lam_ref, lam_init):
    v = lam_ref[...]
    d1 = jnp.sum(v[0:1, :] * v[1:2, :], axis=-1, keepdims=True)
    d2 = jnp.sum(v[2:3, :] * v[3:4, :], axis=-1, keepdims=True)
    return jnp.exp(d1) - jnp.exp(d2) + lam_init


V_ROWS_PAD = 16


def _flash_body(lam_ref, q_ref, k_ref, vt_ref, subln_ref, o_ref, scratch, *, t, hb, unroll, lam_init, tail_work):
    qi = pl.program_id(2)
    m_s, acc_s = scratch[:hb], scratch[hb:]
    q2 = []
    for s in range(hb):
        q = q_ref[:, s * V7X_LANES:(s + 1) * V7X_LANES]
        lane = lax.broadcasted_iota(jnp.int32, q.shape, 1)
        zero = jnp.zeros_like(q)
        q2.append(jnp.concatenate([jnp.where(lane < ATTN_HEAD_DIM, q, zero),
                                   jnp.where(lane >= ATTN_HEAD_DIM, q, zero)], axis=0))
        m_s[s][...] = jnp.full(m_s[s].shape, NEG_BIG, F32)
        acc_s[s][...] = jnp.zeros(acc_s[s].shape, F32)

    def scores(j):
        start = pl.multiple_of(j * t, t)
        return tuple(_dot_nt(k_ref[pl.ds(start, t), s * V7X_LANES:(s + 1) * V7X_LANES], q2[s])
                     for s in range(hb))

    def accumulate(j, scs, masked):
        start = pl.multiple_of(j * t, t)
        for s in range(hb):
            sc = scs[s]
            if masked:
                kv = lax.broadcasted_iota(jnp.int32, sc.shape, 0)
                qq = lax.broadcasted_iota(jnp.int32, sc.shape, 1)
                qq = jnp.where(qq >= t, qq - t, qq)
                sc = jnp.where(kv <= qq, sc, NEG_BIG)
            m_old = m_s[s][...]
            m_new = jnp.maximum(m_old, jnp.max(sc, axis=0, keepdims=True))
            alpha = jnp.exp2(m_old - m_new)
            p = jnp.exp2(sc - m_new).astype(BF16)
            acc_s[s][...] = alpha * acc_s[s][...] + _dot(vt_ref[s, :, pl.ds(start, t)], p)
            m_s[s][...] = m_new

    def run_blocks(j0, n, masked_last):
        scs = scores(j0)
        for u in range(n):
            nxt = scores(j0 + u + 1) if u + 1 < n else None
            accumulate(j0 + u, scs, masked_last and u == n - 1)
            scs = nxt

    def body(g, carry):
        run_blocks(g * unroll, unroll, False)
        return carry

    n_groups = qi // unroll
    lax.fori_loop(0, n_groups, body, 0)
    rem = qi - n_groups * unroll
    for r in range(unroll):
        @pl.when(rem == r)
        def _(r=r):
            run_blocks(n_groups * unroll, r + 1, True)
            tail_work()

    lam = _diff_lambda(lam_ref, lam_init)
    gain = subln_ref[...] * (1.0 - lam_init)
    for s in range(hb):
        acc = acc_s[s][...]
        acc = acc[:V7X_LANES, :] * (1.0 / acc[V7X_LANES:V7X_LANES + 1, :])
        o_t = acc[:, :t] - lam * acc[:, t:]
        ms = jnp.mean(o_t * o_t, axis=0, keepdims=True)
        o_t = o_t * lax.rsqrt(ms + RMS_EPS)
        o_ref[:, s * V7X_LANES:(s + 1) * V7X_LANES] = (o_t.T * gain).astype(BF16)


def _decode_body(lam_ref, q_ref, kn_ref, vn_ref, subln_ref, k_refs, v_refs, o_ref, s, *, lam_init):
    n_pages = len(k_refs)
    _, page, n_heads, hd = k_refs[0].shape
    rows = 2 * n_heads
    prow = page * n_heads
    q8 = q_ref[s].astype(F32)
    lane = lax.broadcasted_iota(jnp.int32, q8.shape, 1)
    q16 = jnp.concatenate([jnp.where(lane < ATTN_HEAD_DIM, q8, 0.0),
                           jnp.where(lane >= ATTN_HEAD_DIM, q8, 0.0)], axis=0)
    q16b = q16.astype(BF16)
    r_i = lax.broadcasted_iota(jnp.int32, (rows, prow), 0)
    c_i = lax.broadcasted_iota(jnp.int32, (rows, prow), 1)
    own = (r_i % n_heads) == (c_i % n_heads)

    s_parts = []
    for p in range(n_pages):
        kp = k_refs[p][0].reshape(prow, hd).astype(BF16)
        s_parts.append(jnp.where(own, _dot_nt(q16b, kp), NEG_BIG))
    kn = kn_ref[s]
    kn2 = jnp.concatenate([kn, kn], axis=0)
    s_new = jnp.sum(q16 * kn2, axis=1, keepdims=True)
    m = s_new
    for sp in s_parts:
        m = jnp.maximum(m, jnp.max(sp, axis=1, keepdims=True))
    p_new = jnp.exp2(s_new - m)
    vn = vn_ref[s]
    acc = p_new * jnp.concatenate([vn, vn], axis=0)
    denom = p_new
    for p in range(n_pages):
        pp = jnp.exp2(s_parts[p] - m)
        denom = denom + jnp.sum(pp, axis=1, keepdims=True)
        acc = acc + _dot(pp.astype(BF16), v_refs[p][0].reshape(prow, hd).astype(BF16))

    lam = _diff_lambda(lam_ref, lam_init)
    acc = acc * (1.0 / denom)
    d = acc[:n_heads, :] - lam * acc[n_heads:, :]
    ms = jnp.mean(d * d, axis=1, keepdims=True)
    o_ref[s] = d * lax.rsqrt(ms + RMS_EPS) * (subln_ref[...] * (1.0 - lam_init))


def _attn_kernel(pt_ref, lam_ref, q_ref, k_ref, vt_ref, subln_ref, qd_ref, kn_ref, vn_ref, *rest,
                 n_pages, spp, t, hb, unroll, lam_init):
    del pt_ref
    n_pg = spp * n_pages
    k_refs, v_refs = rest[:n_pg], rest[n_pg:2 * n_pg]
    o_ref, od_ref = rest[2 * n_pg], rest[2 * n_pg + 1]
    scratch = rest[2 * n_pg + 2:]
    def decode():
        for s in range(spp):
            _decode_body(lam_ref, qd_ref, kn_ref, vn_ref, subln_ref,
                         k_refs[s * n_pages:(s + 1) * n_pages], v_refs[s * n_pages:(s + 1) * n_pages],
                         od_ref, s, lam_init=lam_init)

    _flash_body(lam_ref, q_ref, k_ref, vt_ref, subln_ref, o_ref, scratch,
                t=t, hb=hb, unroll=unroll, lam_init=lam_init, tail_work=decode)


def diff_attn(page_table, lam_vecs, q, k, v_t, subln, q_dec, k_new, v_new, cache_k, cache_v, *,
              bsz, seqlen, n_heads, lam_init, t=256, hb=4, unroll=2):
    nq = seqlen // t
    n_hg = n_heads // hb
    n_steps = bsz * n_hg * nq
    n, n_pages = page_table.shape
    spp, ragged = divmod(n, n_steps)
    assert spp >= 1 and ragged == 0, "sample sequences must tile the prompt attention grid"
    _, page, _, hd = cache_k.shape
    vr = v_t.shape[2]
    w = hb * V7X_LANES
    step = lambda b, h, i: (b * n_hg + h) * nq + i
    qblk = lambda b, h, i, pt: (b * nq + i, h)
    dec = lambda b, h, i, pt: (step(b, h, i), 0, 0)
    const2 = lambda b, h, i, pt: (0, 0)
    page_specs = [pl.BlockSpec((1, page, n_heads, hd),
                               functools.partial(lambda b, h, i, pt, s, p: (pt[step(b, h, i) * spp + s, p], 0, 0, 0),
                                                 s=s, p=p))
                  for s in range(spp) for p in range(n_pages)]
    resident = dict(pipeline_mode=pl.Buffered(1))
    vm = (2 * 2 * t * w * 2 + seqlen * w * 2 + hb * vr * seqlen * 2 + hb * (vr + 8) * 2 * t * 4
          + hb * 4 * t * 2 * t * 4
          + 2 * 2 * spp * n_pages * page * n_heads * hd * 4 + 4 * 2 * n_heads * page * n_heads * n_pages * 4)
    grid_spec = pltpu.PrefetchScalarGridSpec(
        num_scalar_prefetch=1,
        grid=(bsz, n_hg, nq),
        in_specs=[pl.BlockSpec(lam_vecs.shape, const2),
                  pl.BlockSpec((t, w), qblk),
                  pl.BlockSpec((seqlen, w), lambda b, h, i, pt: (b, h), **resident),
                  pl.BlockSpec((None, hb, vr, seqlen), lambda b, h, i, pt: (b, h, 0, 0), **resident),
                  pl.BlockSpec(subln.shape, const2),
                  pl.BlockSpec((spp, n_heads, hd), dec), pl.BlockSpec((spp, n_heads, hd), dec),
                  pl.BlockSpec((spp, n_heads, hd), dec)]
                 + page_specs + page_specs,
        out_specs=[pl.BlockSpec((t, w), qblk), pl.BlockSpec((spp, n_heads, hd), dec)],
        scratch_shapes=[pltpu.VMEM((1, 2 * t), F32)] * hb + [pltpu.VMEM((vr, 2 * t), F32)] * hb,
    )
    return pl.pallas_call(
        functools.partial(_attn_kernel, n_pages=n_pages, spp=spp, t=t, hb=hb, unroll=unroll, lam_init=lam_init),
        grid_spec=grid_spec,
        out_shape=[jax.ShapeDtypeStruct(q.shape, BF16), jax.ShapeDtypeStruct((n, n_heads, hd), F32)],
        compiler_params=_params(("parallel", "parallel", "arbitrary"), vm),
        name="diff_attn",
    )(page_table, lam_vecs, q, k, v_t, subln, q_dec, k_new, v_new,
      *([cache_k] * (spp * n_pages)), *([cache_v] * (spp * n_pages)))


def _rope_tables(pos):
    half = ROT_DIM // 2
    inv = ROPE_THETA ** (-jnp.arange(0, ROT_DIM, 2, dtype=F32) / ROT_DIM)
    ang = pos.astype(F32)[:, None] * inv[None, :]
    cos, sin = jnp.cos(ang), jnp.sin(ang)
    n = pos.shape[0]
    pad = jnp.zeros((n, ATTN_HEAD_DIM - ROT_DIM), F32)
    zeros = jnp.zeros((n, half), F32)
    c_map = jnp.concatenate([cos, cos, pad + 1.0], axis=1)
    sa_map = jnp.concatenate([-sin, zeros, pad], axis=1)
    sb_map = jnp.concatenate([zeros, sin, pad], axis=1)
    return tuple(jnp.concatenate([t, t], axis=1) for t in (c_map, sa_map, sb_map))


def _row(v, width=None):
    v = v.astype(F32).reshape(1, -1)
    if width is not None and v.shape[1] < width:
        v = jnp.pad(v, ((0, 0), (0, width - v.shape[1])))
    return v


def kernel(x_prompt, x_sample, state_ssm, state_conv, cache_k, cache_v, page_table, ssd_norm_pre, ssd_norm_post, ssd_w_in, ssd_conv_w, ssd_conv_b, ssd_dt_bias, ssd_a_log, ssd_d, ssd_gate_norm, ssd_w_out, mlp_norm_pre, mlp_norm_post, mlp_w_up, mlp_w_down, kv_norm, w_kv, attn_norm_pre, attn_norm_post, attn_w_q, attn_lambda_q1, attn_lambda_k1, attn_lambda_q2, attn_lambda_k2, attn_subln, attn_w_o):
    bsz, seqlen, d_model = x_prompt.shape
    n_dec = x_sample.shape[0]
    n_ssd = ssd_w_in.shape[0]
    depth = mlp_w_up.shape[0]
    n_heads_ssd = ssd_a_log.shape[1]
    d_inner = n_heads_ssd * SSD_HEAD_DIM
    conv_dim = ssd_conv_w.shape[2]
    qk_dim = attn_w_q.shape[2]
    n_heads = qk_dim // V7X_LANES
    m_p = bsz * seqlen

    hp = x_prompt.reshape(m_p, d_model)
    hs = x_sample.reshape(n_dec, d_model)

    tm_p = 512
    tm_mlp = 512
    tm_s = n_dec

    L = SSD_CHUNK
    tri = (jnp.arange(L)[:, None] >= jnp.arange(L)[None, :]).astype(F32)
    hrow = jnp.arange(V7X_LANES)[:, None]
    e64 = (hrow == (jnp.arange(d_inner)[None, :] // SSD_HEAD_DIM)).astype(BF16)

    tab_p = _rope_tables(jnp.arange(seqlen))
    past_len = page_table.shape[1] * cache_k.shape[1]
    tab_s = tuple(jnp.broadcast_to(t, (n_dec, V7X_LANES)) for t in _rope_tables(jnp.full((1,), past_len)))

    ssm_p, conv_p, conv_s = [], [], []
    ssm_s = None
    k_p = v_p = k_s = v_s = None
    kb_p = vt_p = None

    for i in range(depth):
        if i < n_ssd:
            j = i
            w_in = ssd_w_in[j]
            w_in_b = jnp.concatenate(
                [w_in[:, :d_inner + conv_dim],
                 jnp.pad(w_in[:, d_inner + conv_dim:], ((0, 0), (0, V7X_LANES - n_heads_ssd)))], axis=1).astype(BF16)
            w_out_b = ssd_w_out[j].astype(BF16)
            g_pre = _row(ssd_norm_pre[j])
            g_post = _row(ssd_norm_post[j])
            cw = ssd_conv_w[j].astype(F32)
            cb = _row(ssd_conv_b[j])
            dtb = _row(ssd_dt_bias[j], V7X_LANES)
            alog = _row(ssd_a_log[j], V7X_LANES)
            dexp = jnp.repeat(ssd_d[j].astype(F32), SSD_HEAD_DIM).reshape(1, d_inner)
            gn = _row(ssd_gate_norm[j])

            z, xbc, dtr = ssd_in_proj(hp, g_pre, w_in_b, d_inner=d_inner, conv_dim=conv_dim, tm=tm_p)
            g, sp, cp = ssd_prompt(z, xbc, dtr, cw, cb, dtb, alog, dexp, gn, (tri, e64),
                                   bsz=bsz, seqlen=seqlen, n_heads=n_heads_ssd)
            hp = out_proj_residual(g, w_out_b, g_post, hp, tm=tm_p)
            ssm_p.append(sp.reshape(bsz, n_heads_ssd, SSD_HEAD_DIM, SSD_D_STATE))
            conv_p.append(cp[:, V7X_SUBLANES - (SSD_D_CONV - 1):, :])

            zs, xbcs, dtrs = ssd_in_proj(hs, g_pre, w_in_b, d_inner=d_inner, conv_dim=conv_dim, tm=tm_s)
            cs_t = jnp.transpose(state_conv[j].astype(F32), (1, 0, 2))
            xs, bs_, cs_, dts, ncs = ssd_sample_conv(xbcs, cs_t, dtrs, cw, cb, dtb, d_inner=d_inner)
            x_t = jnp.transpose(xs.reshape(n_dec, n_heads_ssd, SSD_HEAD_DIM), (0, 2, 1))
            dt_h = dts[:, :n_heads_ssd]
            dt_bcast = jnp.broadcast_to(dt_h[:, :, None], (n_dec, n_heads_ssd, V7X_LANES))
            alog_rep = jnp.broadcast_to(ssd_a_log[j].astype(F32)[:, None], (n_heads_ssd, V7X_LANES))
            ssm_s, ys = ssd_sample_state(state_ssm, j, ssm_s, x_t, dt_bcast, dt_h[:, None, :], alog_rep,
                                         bs_[:, None, :], cs_[:, None, :], sb=4)
            gs = ssd_sample_gate(ys.reshape(n_dec, d_inner), xs, zs, dexp, gn)
            hs = out_proj_residual(gs, w_out_b, g_post, hs, tm=tm_s)
            conv_s.append(jnp.transpose(ncs, (1, 0, 2)).astype(state_conv.dtype))
        else:
            j = i - n_ssd
            if j == 0:
                w_kv_b = w_kv.astype(BF16)
                g_kv = _row(kv_norm)
                k_p, v_p, kb_p, vb_p = shared_kv_proj(hp, g_kv, w_kv_b, tab_p, tm=tm_p, n_pos_blocks=seqlen // tm_p)
                k_s, v_s, _, _ = shared_kv_proj(hs, g_kv, w_kv_b, tab_s, tm=tm_s, n_pos_blocks=1)
                vt_p = jnp.transpose(vb_p.reshape(bsz, seqlen, n_heads, V7X_LANES), (0, 2, 3, 1))
                ones_rows = jnp.zeros((bsz, n_heads, V_ROWS_PAD, seqlen), BF16).at[:, :, 0, :].set(1.0)
                vt_p = jnp.concatenate([vt_p, ones_rows], axis=2)
            lam_init = 0.8 - 0.6 * math.exp(-0.3 * i)
            lam_vecs = jnp.pad(
                jnp.stack([attn_lambda_q1[j], attn_lambda_k1[j], attn_lambda_q2[j], attn_lambda_k2[j]]).astype(F32),
                ((0, V7X_SUBLANES - 4), (0, V7X_LANES - ATTN_HEAD_DIM)))
            g_pre = _row(attn_norm_pre[j])
            g_post = _row(attn_norm_post[j])
            w_q_b = attn_w_q[j].astype(BF16)
            w_o_b = attn_w_o[j].astype(BF16)
            subln = _row(attn_subln[j])

            qp = attn_q_proj(hp, g_pre, w_q_b, tab_p, tm=tm_p, n_pos_blocks=seqlen // tm_p)
            qs = attn_q_proj(hs, g_pre, w_q_b, tab_s, tm=tm_s, n_pos_blocks=1)
            heads = lambda a: a.reshape(n_dec, n_heads, V7X_LANES)
            op, os_ = diff_attn(page_table, lam_vecs, qp, kb_p, vt_p, subln, heads(qs), heads(k_s), heads(v_s),
                                cache_k, cache_v, bsz=bsz, seqlen=seqlen, n_heads=n_heads, lam_init=lam_init)
            hp = out_proj_residual(op, w_o_b, g_post, hp, tm=tm_p)
            hs = out_proj_residual(os_.reshape(n_dec, qk_dim), w_o_b, g_post, hs, tm=tm_s)

        g1 = _row(mlp_norm_pre[i])
        g2 = _row(mlp_norm_post[i])
        wu = mlp_w_up[i].astype(BF16)
        wd = mlp_w_down[i].astype(BF16)
        hp = mlp_residual(hp, g1, wu, wd, g2, tm=tm_mlp)
        hs = mlp_residual(hs, g1, wu, wd, g2, tm=tm_s)

    y_prompt = hp.reshape(bsz, seqlen, d_model)
    y_sample = hs.reshape(n_dec, 1, d_model)
    p_k = k_p.reshape(bsz, seqlen, n_heads, V7X_LANES)
    p_v = v_p.reshape(bsz, seqlen, n_heads, V7X_LANES)
    s_k = k_s.reshape(n_dec, 1, n_heads, V7X_LANES)
    s_v = v_s.reshape(n_dec, 1, n_heads, V7X_LANES)
    return (y_prompt, y_sample, jnp.stack(ssm_p), jnp.stack(conv_p), p_k, p_v,
            ssm_s.astype(state_ssm.dtype), jnp.stack(conv_s), s_k, s_v)
```

```python
import functools
import math

import jax
import jax.numpy as jnp
from jax import lax
from jax.experimental import pallas as pl
from jax.experimental.pallas import tpu as pltpu

F32 = jnp.float32
BF16 = jnp.bfloat16

RMS_EPS = 1e-6
ROPE_THETA = 500000.0

V7X_LANES = 128
V7X_SUBLANES = 8
V7X_VMEM_BYTES = 64 * 1024 * 1024

SSD_HEAD_DIM = 64
SSD_N_GROUPS = 4
SSD_D_STATE = 128
SSD_D_CONV = 4
SSD_CHUNK = 128
ATTN_HEAD_DIM = 64
ROT_DIM = ATTN_HEAD_DIM // 4

NEG_BIG = -1e30


def _vmem_limit(nbytes):
    return int(min(V7X_VMEM_BYTES * 7 // 8, max(32 * 1024 * 1024, nbytes * 3 // 2)))


def _params(sem, vmem_bytes):
    return pltpu.CompilerParams(dimension_semantics=sem, vmem_limit_bytes=_vmem_limit(vmem_bytes))


def _const_spec(shape):
    nd = len(shape)
    return pl.BlockSpec(shape, lambda *_: (0,) * nd, pipeline_mode=pl.Buffered(1))


def _layer_spec(w, layer):
    _, rows, cols = w.shape
    return pl.BlockSpec((None, rows, cols), lambda *_: (layer, 0, 0), pipeline_mode=pl.Buffered(1))


def _whole_spec(shape):
    nd = len(shape)
    return pl.BlockSpec(shape, lambda *_: (0,) * nd)


def _rms(x, g):
    ms = jnp.mean(x * x, axis=-1, keepdims=True)
    return x * lax.rsqrt(ms + RMS_EPS) * g


def _silu(x):
    return x * (1.0 / (1.0 + jnp.exp(-x)))


def _softplus(x):
    return jnp.maximum(x, 0.0) + jnp.log1p(jnp.exp(-jnp.abs(x)))


def _dot(a, b):
    return jnp.dot(a, b, preferred_element_type=F32)


def _dot_nt(a, b):
    return lax.dot_general(a, b, (((1,), (1,)), ((), ())), preferred_element_type=F32)


def _rope(t, cos, sa, sb):
    up = pltpu.roll(t, V7X_LANES - ROT_DIM // 2, 1)
    dn = pltpu.roll(t, ROT_DIM // 2, 1)
    return t * cos + up * sa + dn * sb


def _inproj_kernel(x_ref, g_ref, w_ref, wdt_ref, z_ref, xbc_ref, dt_ref, *, d_inner):
    xn = _rms(x_ref[...], g_ref[...]).astype(BF16)
    z_ref[...] = _dot(xn, w_ref[:, 0:d_inner])
    xbc_ref[...] = _dot(xn, w_ref[:, d_inner:])
    dt_ref[...] = _dot(xn, wdt_ref[...])


def ssd_in_proj(x, g, w, w_dt, layer, *, d_inner, tm):
    m, d = x.shape
    conv_dim = w.shape[-1] - d_inner
    dtw = w_dt.shape[-1]
    n = d_inner + conv_dim + dtw
    vm = 2 * tm * d * 4 + d * n * 2 + 2 * tm * n * 4
    return pl.pallas_call(
        functools.partial(_inproj_kernel, d_inner=d_inner),
        grid=(m // tm,),
        in_specs=[pl.BlockSpec((tm, d), lambda i: (i, 0)), _const_spec((1, d)),
                  _layer_spec(w, layer), _layer_spec(w_dt, layer)],
        out_specs=[pl.BlockSpec((tm, d_inner), lambda i: (i, 0)),
                   pl.BlockSpec((tm, conv_dim), lambda i: (i, 0)),
                   pl.BlockSpec((tm, dtw), lambda i: (i, 0))],
        out_shape=[jax.ShapeDtypeStruct((m, d_inner), F32),
                   jax.ShapeDtypeStruct((m, conv_dim), F32),
                   jax.ShapeDtypeStruct((m, dtw), F32)],
        compiler_params=_params(("parallel",), vm),
        name="ssd_in_proj",
    )(x, g, w, w_dt)


def _qproj_kernel(x_ref, g_ref, w_ref, cos_ref, sa_ref, sb_ref, q_ref, *, n_heads, scale):
    xn = _rms(x_ref[...], g_ref[...]).astype(BF16)
    cos, sa, sb = cos_ref[...], sa_ref[...], sb_ref[...]
    t = _dot(xn, w_ref[...])
    for h in range(n_heads):
        sl = slice(h * V7X_LANES, (h + 1) * V7X_LANES)
        q_ref[:, sl] = (_rope(t[:, sl], cos, sa, sb) * scale).astype(BF16)


def attn_q_proj(x, g, w, layer, tables, *, tm, n_pos_blocks):
    m, d = x.shape
    n = w.shape[-1]
    n_heads = n // V7X_LANES
    tab_spec = pl.BlockSpec((tm, V7X_LANES), lambda i: (i % n_pos_blocks, 0))
    vm = 2 * tm * d * 4 + d * n * 2 + 2 * tm * n * 2 + tm * n * 4 + 6 * tm * V7X_LANES * 4
    return pl.pallas_call(
        functools.partial(_qproj_kernel, n_heads=n_heads, scale=ATTN_HEAD_DIM ** -0.5 * math.log2(math.e)),
        grid=(m // tm,),
        in_specs=[pl.BlockSpec((tm, d), lambda i: (i, 0)), _const_spec((1, d)), _layer_spec(w, layer),
                  tab_spec, tab_spec, tab_spec],
        out_specs=pl.BlockSpec((tm, n), lambda i: (i, 0)),
        out_shape=jax.ShapeDtypeStruct((m, n), BF16),
        compiler_params=_params(("parallel",), vm),
        name="attn_q_proj",
    )(x, g, w, *tables)


def _kvproj_kernel(x_ref, g_ref, w_ref, cos_ref, sa_ref, sb_ref, k_ref, v_ref, kb_ref, vb_ref, *, n_heads):
    xn = _rms(x_ref[...], g_ref[...]).astype(BF16)
    cos, sa, sb = cos_ref[...], sa_ref[...], sb_ref[...]
    qk = n_heads * V7X_LANES
    t = _dot(xn, w_ref[:, :qk])
    for h in range(n_heads):
        sl = slice(h * V7X_LANES, (h + 1) * V7X_LANES)
        k = _rope(t[:, sl], cos, sa, sb)
        k_ref[:, sl] = k
        kb_ref[:, sl] = k.astype(BF16)
    v = _dot(xn, w_ref[:, qk:])
    v_ref[...] = v
    vb_ref[...] = v.astype(BF16)


def shared_kv_proj(x, g, w, tables, *, tm, n_pos_blocks):
    m, d = x.shape
    n = w.shape[-1]
    qk = n // 2
    n_heads = qk // V7X_LANES
    tab_spec = pl.BlockSpec((tm, V7X_LANES), lambda i: (i % n_pos_blocks, 0))
    row = lambda i: (i, 0)
    vm = 2 * tm * d * 4 + d * n * 2 + 2 * tm * n * 6 + tm * n * 4 + 6 * tm * V7X_LANES * 4
    return pl.pallas_call(
        functools.partial(_kvproj_kernel, n_heads=n_heads),
        grid=(m // tm,),
        in_specs=[pl.BlockSpec((tm, d), row), _const_spec((1, d)), _layer_spec(w, 0),
                  tab_spec, tab_spec, tab_spec],
        out_specs=[pl.BlockSpec((tm, qk), row)] * 4,
        out_shape=[jax.ShapeDtypeStruct((m, qk), F32), jax.ShapeDtypeStruct((m, qk), F32),
                   jax.ShapeDtypeStruct((m, qk), BF16), jax.ShapeDtypeStruct((m, qk), BF16)],
        compiler_params=_params(("parallel",), vm),
        name="shared_kv_proj",
    )(x, g, w, *tables)


def _mixer_out_mlp_kernel(a_ref, wo_ref, go_ref, h_ref, g1_ref, wu_ref, wd_ref, g2_ref, o_ref, *, tf):
    y = _dot(a_ref[...].astype(BF16), wo_ref[...])
    h = h_ref[...] + _rms(y, go_ref[...])
    xn = _rms(h, g1_ref[...]).astype(BF16)
    d_ff = wu_ref.shape[1]
    acc = jnp.zeros(h.shape, F32)
    for f in range(d_ff // tf):
        u = jnp.maximum(_dot(xn, wu_ref[:, f * tf:(f + 1) * tf]), 0.0)
        acc = acc + _dot((u * u).astype(BF16), wd_ref[f * tf:(f + 1) * tf, :])
    o_ref[...] = h + _rms(acc, g2_ref[...])


def mixer_out_mlp(a, w_o, o_layer, g_o, h, g_pre, w_up, w_down, layer, g_post, *, tm, tf=512):
    m, k = a.shape
    d = h.shape[1]
    d_ff = w_up.shape[-1]
    row = lambda i: (i, 0)
    vm = (2 * tm * k * a.dtype.itemsize + k * d * 2 + 4 * tm * d * 4 + 2 * d * d_ff * 2
          + 3 * tm * d * 4 + 2 * tm * tf * 4)
    return pl.pallas_call(
        functools.partial(_mixer_out_mlp_kernel, tf=tf),
        grid=(m // tm,),
        in_specs=[pl.BlockSpec((tm, k), row), _layer_spec(w_o, o_layer), _const_spec((1, d)),
                  pl.BlockSpec((tm, d), row), _const_spec((1, d)), _layer_spec(w_up, layer),
                  _layer_spec(w_down, layer), _const_spec((1, d))],
        out_specs=pl.BlockSpec((tm, d), row),
        out_shape=jax.ShapeDtypeStruct((m, d), F32),
        compiler_params=_params(("parallel",), vm),
        name="mixer_out_mlp",
    )(a, w_o, g_o, h, g_pre, w_up, w_down, g_post)


def _expand3(v, e):
    hi = v.astype(BF16)
    r1 = v - hi.astype(F32)
    mid = r1.astype(BF16)
    lo = (r1 - mid.astype(F32)).astype(BF16)
    return _dot(hi, e) + _dot(mid, e) + _dot(lo, e)


def _gated_group_norm(y, z, gate_norm, n_groups):
    g = y * _silu(z)
    width = g.shape[1] // n_groups
    outs = []
    for i in range(n_groups):
        gg = g[:, i * width:(i + 1) * width]
        ms = jnp.mean(gg * gg, axis=-1, keepdims=True)
        outs.append(gg * lax.rsqrt(ms + RMS_EPS) * gate_norm[:, i * width:(i + 1) * width])
    return outs


def _ssd_prompt_kernel(z_ref, xbc_ref, dtr_ref, cw_ref, cb_ref, dtb_ref, alog_ref, dexp_ref, gn_ref,
                       tri_ref, e64_ref,
                       g_ref, ssm_ref, conv_ref,
                       cbuf, st_t, y_s, *, n_heads, d_inner):
    L = SSD_CHUNK
    N = SSD_D_STATE
    gn_w = SSD_N_GROUPS * N
    heads_per_group = n_heads // SSD_N_GROUPS
    c = pl.program_id(1)
    last = pl.num_programs(1) - 1
    halo = V7X_SUBLANES

    @pl.when(c == 0)
    def _():
        cbuf[0:halo, :] = jnp.zeros((halo, cbuf.shape[1]), F32)
        st_t[...] = jnp.zeros(st_t.shape, F32)

    cbuf[halo:halo + L, :] = xbc_ref[...]
    xw = cbuf[...]
    acc = cb_ref[...] + xw[halo:, :] * cw_ref[SSD_D_CONV - 1:SSD_D_CONV, :]
    for k in range(1, SSD_D_CONV):
        acc = acc + pltpu.roll(xw, k, 0)[halo:, :] * cw_ref[SSD_D_CONV - 1 - k:SSD_D_CONV - k, :]
    act = _silu(acc)
    x = act[:, :d_inner]
    cbuf[0:halo, :] = xw[L:, :]

    dt = _softplus(dtr_ref[...] + dtb_ref[...])
    a = -jnp.exp(alog_ref[...])
    acum = jnp.dot(tri_ref[...], a * dt, precision=lax.Precision.HIGHEST, preferred_element_type=F32)
    acum_last = acum[L - 1:L, :]
    acum_t = acum.T
    cd = _expand3(jnp.broadcast_to(jnp.exp(acum_last), (V7X_SUBLANES, V7X_LANES)), e64_ref[...])[0:1, :]
    e64 = e64_ref[...]
    dt_x = _dot(dt.astype(BF16), e64)
    w_x = _dot((jnp.exp(acum_last - acum) * dt).astype(BF16), e64)
    e1_x = _dot(jnp.exp(acum).astype(BF16), e64)
    xdt = (x * dt_x).astype(BF16)
    xs = (x * w_x).astype(BF16)

    li = lax.broadcasted_iota(jnp.int32, (L, L), 0)
    si = lax.broadcasted_iota(jnp.int32, (L, L), 1)
    causal = li >= si
    lo_half = lax.broadcasted_iota(jnp.int32, (L, V7X_LANES), 1) < SSD_HEAD_DIM

    for grp in range(SSD_N_GROUPS):
        b_g = act[:, d_inner + grp * N:d_inner + (grp + 1) * N]
        c_g = act[:, d_inner + gn_w + grp * N:d_inner + gn_w + (grp + 1) * N].astype(BF16)
        cb_g = jnp.where(causal, _dot_nt(c_g, b_g.astype(BF16)), 0.0)
        bt_g = b_g.T.astype(BF16)
        for pr in range(heads_per_group // 2):
            q = grp * (heads_per_group // 2) + pr
            sl = slice(q * V7X_LANES, (q + 1) * V7X_LANES)
            st_prev = st_t[:, sl]
            y_off = _dot(c_g, st_prev.astype(BF16)) * e1_x[:, sl]
            st_t[:, sl] = st_prev * cd[:, sl] + _dot(bt_g, xs[:, sl])
            xp = xdt[:, sl]
            ys = []
            for h in (2 * q, 2 * q + 1):
                seg = jnp.minimum(jnp.broadcast_to(acum[:, h:h + 1], (L, L)) - acum_t[h:h + 1, :], 0.0)
                ys.append(_dot((jnp.exp(seg) * cb_g).astype(BF16), xp))
            y_s[:, sl] = jnp.where(lo_half, ys[0], ys[1]) + y_off

    y = y_s[...] + x * dexp_ref[...]
    outs = _gated_group_norm(y, z_ref[...], gn_ref[...], SSD_N_GROUPS)
    width = d_inner // SSD_N_GROUPS
    for i, o in enumerate(outs):
        g_ref[:, i * width:(i + 1) * width] = o.astype(BF16)

    @pl.when(c == last)
    def _():
        conv_ref[...] = cbuf[0:halo, :]
        for q in range(n_heads // 2):
            sl = slice(q * V7X_LANES, (q + 1) * V7X_LANES)
            ssm_ref[sl, :] = st_t[:, sl].T


def ssd_prompt(z, xbc, dtr, cw, cb, dtb, alog, dexp, gn, consts, *, bsz, seqlen, n_heads):
    m, d_inner = z.shape
    conv_dim = xbc.shape[1]
    L = SSD_CHUNK
    nc = seqlen // L
    tri, e64 = consts
    blk = lambda b, c: (b * nc + c, 0)
    vm = (2 * L * (d_inner + conv_dim + V7X_LANES) * 4 + 2 * L * d_inner * 2
          + (L + 8) * conv_dim * 4 + 2 * L * d_inner * 4
          + 2 * e64.size * 2 + 4 * n_heads * SSD_HEAD_DIM * SSD_D_STATE * 4
          + 8 * L * conv_dim * 4)
    return pl.pallas_call(
        functools.partial(_ssd_prompt_kernel, n_heads=n_heads, d_inner=d_inner),
        grid=(bsz, nc),
        in_specs=[pl.BlockSpec((L, d_inner), blk), pl.BlockSpec((L, conv_dim), blk),
                  pl.BlockSpec((L, V7X_LANES), blk),
                  _const_spec(cw.shape), _const_spec(cb.shape), _const_spec(dtb.shape),
                  _const_spec(alog.shape), _const_spec(dexp.shape), _const_spec(gn.shape),
                  _const_spec(tri.shape), _const_spec(e64.shape)],
        out_specs=[pl.BlockSpec((L, d_inner), blk),
                   pl.BlockSpec((None, n_heads * SSD_HEAD_DIM, SSD_D_STATE), lambda b, c: (b, 0, 0)),
                   pl.BlockSpec((None, V7X_SUBLANES, conv_dim), lambda b, c: (b, 0, 0))],
        out_shape=[jax.ShapeDtypeStruct((m, d_inner), BF16),
                   jax.ShapeDtypeStruct((bsz, n_heads * SSD_HEAD_DIM, SSD_D_STATE), F32),
                   jax.ShapeDtypeStruct((bsz, V7X_SUBLANES, conv_dim), F32)],
        scratch_shapes=[pltpu.VMEM((L + V7X_SUBLANES, conv_dim), F32),
                        pltpu.VMEM((SSD_D_STATE, d_inner), F32),
                        pltpu.VMEM((L, d_inner), F32)],
        compiler_params=_params(("arbitrary", "arbitrary"), vm),
        name="ssd_prompt",
    )(z, xbc, dtr, cw, cb, dtb, alog, dexp, gn, tri, e64)


def _ssd_sample_conv_kernel(xbc_ref, cs_ref, dtr_ref, cw_ref, cb_ref, dtb_ref,
                            x_ref, b_ref, c_ref, dt_ref, ncs_ref, *, d_inner):
    new = xbc_ref[...]
    acc = cb_ref[...] + new * cw_ref[SSD_D_CONV - 1:SSD_D_CONV, :]
    for k in range(SSD_D_CONV - 1):
        acc = acc + cs_ref[k] * cw_ref[k:k + 1, :]
    act = _silu(acc)
    gn_w = SSD_N_GROUPS * SSD_D_STATE
    x_ref[...] = act[:, :d_inner]
    b_ref[...] = act[:, d_inner:d_inner + gn_w]
    c_ref[...] = act[:, d_inner + gn_w:]
    dt_ref[...] = _softplus(dtr_ref[...] + dtb_ref[...])
    for k in range(SSD_D_CONV - 2):
        ncs_ref[k] = cs_ref[k + 1]
    ncs_ref[SSD_D_CONV - 2] = new


def ssd_sample_conv(xbc, cs_t, dtr, cw, cb, dtb, *, d_inner):
    n, conv_dim = xbc.shape
    gn_w = SSD_N_GROUPS * SSD_D_STATE
    args = (xbc, cs_t, dtr, cw, cb, dtb)
    vm = 6 * xbc.size * 4 + 4 * cs_t.size * 4
    return pl.pallas_call(
        functools.partial(_ssd_sample_conv_kernel, d_inner=d_inner),
        grid=(1,),
        in_specs=[_whole_spec(a.shape) for a in args],
        out_specs=[_whole_spec((n, d_inner)), _whole_spec((n, gn_w)), _whole_spec((n, gn_w)),
                   _whole_spec(dtr.shape), _whole_spec(cs_t.shape)],
        out_shape=[jax.ShapeDtypeStruct((n, d_inner), F32), jax.ShapeDtypeStruct((n, gn_w), F32),
                   jax.ShapeDtypeStruct((n, gn_w), F32), jax.ShapeDtypeStruct(dtr.shape, F32),
                   jax.ShapeDtypeStruct(cs_t.shape, F32)],
        compiler_params=_params(("arbitrary",), vm),
        name="ssd_sample_conv",
    )(*args)


def _ssd_sample_state_kernel(st_ref, xt_ref, dtb_ref, dtr_ref, alog_ref, b_ref, c_ref, *rest, sb, n_heads):
    nst_ref, y_ref = rest[-2:]
    heads_per_group = n_heads // SSD_N_GROUPS
    N = SSD_D_STATE
    P = SSD_HEAD_DIM
    a_rep = -jnp.exp(alog_ref[...])
    for s in range(sb):
        dec = jnp.exp(a_rep * dtb_ref[s])
        xdt_t = xt_ref[s] * dtr_ref[s]
        for grp in range(SSD_N_GROUPS):
            b_row = b_ref[s][:, grp * N:(grp + 1) * N]
            c_row = c_ref[s][:, grp * N:(grp + 1) * N]
            news = []
            for r in range(heads_per_group):
                h = grp * heads_per_group + r
                new = st_ref[s, h] * dec[h:h + 1, :] + xdt_t[:, h:h + 1] * b_row
                nst_ref[s, h] = new
                news.append(new)
            hg = jnp.concatenate(news, axis=0).astype(BF16)
            c8 = jnp.broadcast_to(c_row, (V7X_SUBLANES, N)).astype(BF16)
            yg = _dot_nt(c8, hg)
            y_ref[s, :, grp * heads_per_group * P:(grp + 1) * heads_per_group * P] = yg[0:1, :]


def ssd_sample_state(states, layer, stacked, x_t, dt_bcast, dt_row, alog_rep, b_in, c_in, *, sb):
    _, n, n_heads, P, N = states.shape
    d_inner = n_heads * P
    gn_w = b_in.shape[-1]
    lyr = lambda i: (layer, i, 0, 0, 0)
    blk3 = lambda i: (i, 0, 0)
    vm = 4 * sb * n_heads * P * N * 4 + 4 * sb * (P * V7X_LANES + n_heads * V7X_LANES) * 4
    args = [states, x_t, dt_bcast, dt_row, alog_rep, b_in, c_in]
    in_specs = [pl.BlockSpec((None, sb, n_heads, P, N), lyr),
                pl.BlockSpec((sb, P, n_heads), blk3),
                pl.BlockSpec((sb, n_heads, V7X_LANES), blk3),
                pl.BlockSpec((sb, 1, n_heads), blk3),
                _const_spec(alog_rep.shape),
                pl.BlockSpec((sb, 1, gn_w), blk3),
                pl.BlockSpec((sb, 1, gn_w), blk3)]
    aliases = {}
    if stacked is not None:
        aliases = {len(args): 0}
        args.append(stacked)
        in_specs.append(pl.BlockSpec(memory_space=pl.ANY))
    return pl.pallas_call(
        functools.partial(_ssd_sample_state_kernel, sb=sb, n_heads=n_heads),
        grid=(n // sb,),
        in_specs=in_specs,
        out_specs=[pl.BlockSpec((None, sb, n_heads, P, N), lyr),
                   pl.BlockSpec((sb, 1, d_inner), blk3)],
        out_shape=[jax.ShapeDtypeStruct(states.shape, F32),
                   jax.ShapeDtypeStruct((n, 1, d_inner), F32)],
        input_output_aliases=aliases,
        compiler_params=_params(("parallel",), vm),
        name="ssd_sample_state",
    )(*args)


def _ssd_sample_gate_kernel(y_ref, x_ref, z_ref, dexp_ref, gn_ref, g_ref):
    y = y_ref[...] + x_ref[...] * dexp_ref[...]
    outs = _gated_group_norm(y, z_ref[...], gn_ref[...], SSD_N_GROUPS)
    width = y.shape[1] // SSD_N_GROUPS
    for i, o in enumerate(outs):
        g_ref[:, i * width:(i + 1) * width] = o.astype(BF16)


def ssd_sample_gate(y, x, z, dexp, gn):
    args = (y, x, z, dexp, gn)
    return pl.pallas_call(
        _ssd_sample_gate_kernel,
        grid=(1,),
        in_specs=[_whole_spec(a.shape) for a in args],
        out_specs=_whole_spec(y.shape),
        out_shape=jax.ShapeDtypeStruct(y.shape, BF16),
        compiler_params=_params(("arbitrary",), 8 * y.size * 4),
        name="ssd_sample_gate",
    )(*args)


def _diff_lambda(lam_ref, lam_init):
    v = lam_ref[...]
    d1 = jnp.sum(v[0:1, :] * v[1:2, :], axis=-1, keepdims=True)
    d2 = jnp.sum(v[2:3, :] * v[3:4, :], axis=-1, keepdims=True)
    return jnp.exp(d1) - jnp.exp(d2) + lam_init


V_ROWS_PAD = 16


def _flash_body(lam_ref, q_ref, k_ref, vt_ref, subln_ref, o_ref, scratch, *, t, hb, unroll, lam_init, tail_work):
    qi = pl.program_id(2)
    m_s, acc_s = scratch[:hb], scratch[hb:]
    q2 = []
    for s in range(hb):
        q = q_ref[:, s * V7X_LANES:(s + 1) * V7X_LANES]
        lane = lax.broadcasted_iota(jnp.int32, q.shape, 1)
        zero = jnp.zeros_like(q)
        q2.append(jnp.concatenate([jnp.where(lane < ATTN_HEAD_DIM, q, zero),
                                   jnp.where(lane >= ATTN_HEAD_DIM, q, zero)], axis=0))
        m_s[s][...] = jnp.full(m_s[s].shape, NEG_BIG, F32)
        acc_s[s][...] = jnp.zeros(acc_s[s].shape, F32)

    def scores(j):
        start = pl.multiple_of(j * t, t)
        return tuple(_dot_nt(k_ref[pl.ds(start, t), s * V7X_LANES:(s + 1) * V7X_LANES], q2[s])
                     for s in range(hb))

    def accumulate(j, scs, masked):
        start = pl.multiple_of(j * t, t)
        for s in range(hb):
            sc = scs[s]
            if masked:
                kv = lax.broadcasted_iota(jnp.int32, sc.shape, 0)
                qq = lax.broadcasted_iota(jnp.int32, sc.shape, 1)
                qq = jnp.where(qq >= t, qq - t, qq)
                sc = jnp.where(kv <= qq, sc, NEG_BIG)
            m_old = m_s[s][...]
            m_new = jnp.maximum(m_old, jnp.max(sc, axis=0, keepdims=True))
            alpha = jnp.exp2(m_old - m_new)
            p = jnp.exp2(sc - m_new).astype(BF16)
            acc_s[s][...] = alpha * acc_s[s][...] + _dot(vt_ref[s, :, pl.ds(start, t)], p)
            m_s[s][...] = m_new

    def run_blocks(j0, n, masked_last):
        scs = scores(j0)
        for u in range(n):
            nxt = scores(j0 + u + 1) if u + 1 < n else None
            accumulate(j0 + u, scs, masked_last and u == n - 1)
            scs = nxt

    def body(g, carry):
        run_blocks(g * unroll, unroll, False)
        return carry

    n_groups = qi // unroll
    lax.fori_loop(0, n_groups, body, 0)
    rem = qi - n_groups * unroll
    for r in range(unroll):
        @pl.when(rem == r)
        def _(r=r):
            run_blocks(n_groups * unroll, r + 1, True)
            tail_work()

    lam = _diff_lambda(lam_ref, lam_init)
    gain = subln_ref[...] * (1.0 - lam_init)
    for s in range(hb):
        acc = acc_s[s][...]
        acc = acc[:V7X_LANES, :] * (1.0 / acc[V7X_LANES:V7X_LANES + 1, :])
        o_t = acc[:, :t] - lam * acc[:, t:]
        ms = jnp.mean(o_t * o_t, axis=0, keepdims=True)
        o_t = o_t * lax.rsqrt(ms + RMS_EPS)
        o_ref[:, s * V7X_LANES:(s + 1) * V7X_LANES] = (o_t.T * gain).astype(BF16)


def _decode_body(lam_ref, q_ref, kn_ref, vn_ref, subln_ref, k_refs, v_refs, o_ref, s, *, lam_init):
    n_pages = len(k_refs)
    _, page, n_heads, hd = k_refs[0].shape
    rows = 2 * n_heads
    prow = page * n_heads
    q8 = q_ref[s].astype(F32)
    lane = lax.broadcasted_iota(jnp.int32, q8.shape, 1)
    q16 = jnp.concatenate([jnp.where(lane < ATTN_HEAD_DIM, q8, 0.0),
                           jnp.where(lane >= ATTN_HEAD_DIM, q8, 0.0)], axis=0)
    q16b = q16.astype(BF16)
    r_i = lax.broadcasted_iota(jnp.int32, (rows, prow), 0)
    c_i = lax.broadcasted_iota(jnp.int32, (rows, prow), 1)
    own = (r_i % n_heads) == (c_i % n_heads)

    s_parts = []
    for p in range(n_pages):
        kp = k_refs[p][0].reshape(prow, hd).astype(BF16)
        s_parts.append(jnp.where(own, _dot_nt(q16b, kp), NEG_BIG))
    kn = kn_ref[s]
    kn2 = jnp.concatenate([kn, kn], axis=0)
    s_new = jnp.sum(q16 * kn2, axis=1, keepdims=True)
    m = s_new
    for sp in s_parts:
        m = jnp.maximum(m, jnp.max(sp, axis=1, keepdims=True))
    p_new = jnp.exp2(s_new - m)
    vn = vn_ref[s]
    acc = p_new * jnp.concatenate([vn, vn], axis=0)
    denom = p_new
    for p in range(n_pages):
        pp = jnp.exp2(s_parts[p] - m)
        denom = denom + jnp.sum(pp, axis=1, keepdims=True)
        acc = acc + _dot(pp.astype(BF16), v_refs[p][0].reshape(prow, hd).astype(BF16))

    lam = _diff_lambda(lam_ref, lam_init)
    acc = acc * (1.0 / denom)
    d = acc[:n_heads, :] - lam * acc[n_heads:, :]
    ms = jnp.mean(d * d, axis=1, keepdims=True)
    o_ref[s] = d * lax.rsqrt(ms + RMS_EPS) * (subln_ref[...] * (1.0 - lam_init))


def _attn_kernel(pt_ref, lam_ref, q_ref, k_ref, vt_ref, subln_ref, qd_ref, kn_ref, vn_ref, *rest,
                 n_pages, spp, t, hb, unroll, lam_init):
    del pt_ref
    n_pg = spp * n_pages
    k_refs, v_refs = rest[:n_pg], rest[n_pg:2 * n_pg]
    o_ref, od_ref = rest[2 * n_pg], rest[2 * n_pg + 1]
    scratch = rest[2 * n_pg + 2:]
    def decode():
        for s in range(spp):
            _decode_body(lam_ref, qd_ref, kn_ref, vn_ref, subln_ref,
                         k_refs[s * n_pages:(s + 1) * n_pages], v_refs[s * n_pages:(s + 1) * n_pages],
                         od_ref, s, lam_init=lam_init)

    _flash_body(lam_ref, q_ref, k_ref, vt_ref, subln_ref, o_ref, scratch,
                t=t, hb=hb, unroll=unroll, lam_init=lam_init, tail_work=decode)


def diff_attn(page_table, lam_vecs, q, k, v_t, subln, q_dec, k_new, v_new, cache_k, cache_v, *,
              bsz, seqlen, n_heads, lam_init, t=256, hb=4, unroll=2):
    nq = seqlen // t
    n_hg = n_heads // hb
    n_steps = bsz * n_hg * nq
    n, n_pages = page_table.shape
    spp, ragged = divmod(n, n_steps)
    assert spp >= 1 and ragged == 0, "sample sequences must tile the prompt attention grid"
    _, page, _, hd = cache_k.shape
    vr = v_t.shape[2]
    w = hb * V7X_LANES
    step = lambda b, h, i: (b * n_hg + h) * nq + i
    qblk = lambda b, h, i, pt: (b * nq + i, h)
    dec = lambda b, h, i, pt: (step(b, h, i), 0, 0)
    const2 = lambda b, h, i, pt: (0, 0)
    page_specs = [pl.BlockSpec((1, page, n_heads, hd),
                               functools.partial(lambda b, h, i, pt, s, p: (pt[step(b, h, i) * spp + s, p], 0, 0, 0),
                                                 s=s, p=p))
                  for s in range(spp) for p in range(n_pages)]
    resident = dict(pipeline_mode=pl.Buffered(1))
    vm = (2 * 2 * t * w * 2 + seqlen * w * 2 + hb * vr * seqlen * 2 + hb * (vr + 8) * 2 * t * 4
          + hb * 4 * t * 2 * t * 4
          + 2 * 2 * spp * n_pages * page * n_heads * hd * 4 + 4 * 2 * n_heads * page * n_heads * n_pages * 4)
    grid_spec = pltpu.PrefetchScalarGridSpec(
        num_scalar_prefetch=1,
        grid=(bsz, n_hg, nq),
        in_specs=[pl.BlockSpec(lam_vecs.shape, const2),
                  pl.BlockSpec((t, w), qblk),
                  pl.BlockSpec((seqlen, w), lambda b, h, i, pt: (b, h), **resident),
                  pl.BlockSpec((None, hb, vr, seqlen), lambda b, h, i, pt: (b, h, 0, 0), **resident),
                  pl.BlockSpec(subln.shape, const2),
                  pl.BlockSpec((spp, n_heads, hd), dec), pl.BlockSpec((spp, n_heads, hd), dec),
                  pl.BlockSpec((spp, n_heads, hd), dec)]
                 + page_specs + page_specs,
        out_specs=[pl.BlockSpec((t, w), qblk), pl.BlockSpec((spp, n_heads, hd), dec)],
        scratch_shapes=[pltpu.VMEM((1, 2 * t), F32)] * hb + [pltpu.VMEM((vr, 2 * t), F32)] * hb,
    )
    return pl.pallas_call(
        functools.partial(_attn_kernel, n_pages=n_pages, spp=spp, t=t, hb=hb, unroll=unroll, lam_init=lam_init),
        grid_spec=grid_spec,
        out_shape=[jax.ShapeDtypeStruct(q.shape, BF16), jax.ShapeDtypeStruct((n, n_heads, hd), F32)],
        compiler_params=_params(("parallel", "parallel", "arbitrary"), vm),
        name="diff_attn",
    )(page_table, lam_vecs, q, k, v_t, subln, q_dec, k_new, v_new,
      *([cache_k] * (spp * n_pages)), *([cache_v] * (spp * n_pages)))


def _rope_tables(pos):
    half = ROT_DIM // 2
    inv = ROPE_THETA ** (-jnp.arange(0, ROT_DIM, 2, dtype=F32) / ROT_DIM)
    ang = pos.astype(F32)[:, None] * inv[None, :]
    cos, sin = jnp.cos(ang), jnp.sin(ang)
    n = pos.shape[0]
    pad = jnp.zeros((n, ATTN_HEAD_DIM - ROT_DIM), F32)
    zeros = jnp.zeros((n, half), F32)
    c_map = jnp.concatenate([cos, cos, pad + 1.0], axis=1)
    sa_map = jnp.concatenate([-sin, zeros, pad], axis=1)
    sb_map = jnp.concatenate([zeros, sin, pad], axis=1)
    return tuple(jnp.concatenate([t, t], axis=1) for t in (c_map, sa_map, sb_map))


def _row(v, width=None):
    v = v.astype(F32).reshape(1, -1)
    if width is not None and v.shape[1] < width:
        v = jnp.pad(v, ((0, 0), (0, width - v.shape[1])))
    return v


def kernel(x_prompt, x_sample, state_ssm, state_conv, cache_k, cache_v, page_table, ssd_norm_pre, ssd_norm_post, ssd_w_in, ssd_conv_w, ssd_conv_b, ssd_dt_bias, ssd_a_log, ssd_d, ssd_gate_norm, ssd_w_out, mlp_norm_pre, mlp_norm_post, mlp_w_up, mlp_w_down, kv_norm, w_kv, attn_norm_pre, attn_norm_post, attn_w_q, attn_lambda_q1, attn_lambda_k1, attn_lambda_q2, attn_lambda_k2, attn_subln, attn_w_o):
    bsz, seqlen, d_model = x_prompt.shape
    n_dec = x_sample.shape[0]
    n_ssd = ssd_w_in.shape[0]
    depth = mlp_w_up.shape[0]
    n_heads_ssd = ssd_a_log.shape[1]
    d_inner = n_heads_ssd * SSD_HEAD_DIM
    conv_dim = ssd_conv_w.shape[2]
    qk_dim = attn_w_q.shape[2]
    n_heads = qk_dim // V7X_LANES
    m_p = bsz * seqlen

    hp = x_prompt.reshape(m_p, d_model)
    hs = x_sample.reshape(n_dec, d_model)

    tm_p = 512
    tm_mlp = 512
    tm_s = n_dec

    L = SSD_CHUNK
    tri = (jnp.arange(L)[:, None] >= jnp.arange(L)[None, :]).astype(F32)
    hrow = jnp.arange(V7X_LANES)[:, None]
    e64 = (hrow == (jnp.arange(d_inner)[None, :] // SSD_HEAD_DIM)).astype(BF16)

    tab_p = _rope_tables(jnp.arange(seqlen))
    past_len = page_table.shape[1] * cache_k.shape[1]
    tab_s = tuple(jnp.broadcast_to(t, (n_dec, V7X_LANES)) for t in _rope_tables(jnp.full((1,), past_len)))

    ssm_p, conv_p, conv_s = [], [], []
    ssm_s = None
    k_p = v_p = k_s = v_s = None
    kb_p = vt_p = None

    w_in_b = ssd_w_in[:, :, :d_inner + conv_dim].astype(BF16)
    w_dt_b = jnp.pad(ssd_w_in[:, :, d_inner + conv_dim:].astype(BF16),
                     ((0, 0), (0, 0), (0, V7X_LANES - n_heads_ssd)))
    w_out_b = ssd_w_out.astype(BF16)
    w_q_b = attn_w_q.astype(BF16)
    w_o_b = attn_w_o.astype(BF16)
    w_kv_b = w_kv.astype(BF16)[None]
    wu = mlp_w_up.astype(BF16)
    wd = mlp_w_down.astype(BF16)

    for i in range(depth):
        if i < n_ssd:
            j = i
            g_pre = _row(ssd_norm_pre[j])
            g_post = _row(ssd_norm_post[j])
            cw = ssd_conv_w[j].astype(F32)
            cb = _row(ssd_conv_b[j])
            dtb = _row(ssd_dt_bias[j], V7X_LANES)
            alog = _row(ssd_a_log[j], V7X_LANES)
            dexp = jnp.repeat(ssd_d[j].astype(F32), SSD_HEAD_DIM).reshape(1, d_inner)
            gn = _row(ssd_gate_norm[j])

            z, xbc, dtr = ssd_in_proj(hp, g_pre, w_in_b, w_dt_b, j, d_inner=d_inner, tm=tm_p)
            g, sp, cp = ssd_prompt(z, xbc, dtr, cw, cb, dtb, alog, dexp, gn, (tri, e64),
                                   bsz=bsz, seqlen=seqlen, n_heads=n_heads_ssd)
            mix_p, mix_s, w_mix = g, None, w_out_b
            ssm_p.append(sp.reshape(bsz, n_heads_ssd, SSD_HEAD_DIM, SSD_D_STATE))
            conv_p.append(cp[:, V7X_SUBLANES - (SSD_D_CONV - 1):, :])

            zs, xbcs, dtrs = ssd_in_proj(hs, g_pre, w_in_b, w_dt_b, j, d_inner=d_inner, tm=tm_s)
            cs_t = jnp.transpose(state_conv[j].astype(F32), (1, 0, 2))
            xs, bs_, cs_, dts, ncs = ssd_sample_conv(xbcs, cs_t, dtrs, cw, cb, dtb, d_inner=d_inner)
            x_t = jnp.transpose(xs.reshape(n_dec, n_heads_ssd, SSD_HEAD_DIM), (0, 2, 1))
            dt_h = dts[:, :n_heads_ssd]
            dt_bcast = jnp.broadcast_to(dt_h[:, :, None], (n_dec, n_heads_ssd, V7X_LANES))
            alog_rep = jnp.broadcast_to(ssd_a_log[j].astype(F32)[:, None], (n_heads_ssd, V7X_LANES))
            ssm_s, ys = ssd_sample_state(state_ssm, j, ssm_s, x_t, dt_bcast, dt_h[:, None, :], alog_rep,
                                         bs_[:, None, :], cs_[:, None, :], sb=4)
            mix_s = ssd_sample_gate(ys.reshape(n_dec, d_inner), xs, zs, dexp, gn)
            conv_s.append(jnp.transpose(ncs, (1, 0, 2)).astype(state_conv.dtype))
        else:
            j = i - n_ssd
            if j == 0:
                g_kv = _row(kv_norm)
                k_p, v_p, kb_p, vb_p = shared_kv_proj(hp, g_kv, w_kv_b, tab_p, tm=tm_p, n_pos_blocks=seqlen // tm_p)
                k_s, v_s, _, _ = shared_kv_proj(hs, g_kv, w_kv_b, tab_s, tm=tm_s, n_pos_blocks=1)
                vt_p = jnp.transpose(vb_p.reshape(bsz, seqlen, n_heads, V7X_LANES), (0, 2, 3, 1))
                ones_rows = jnp.zeros((bsz, n_heads, V_ROWS_PAD, seqlen), BF16).at[:, :, 0, :].set(1.0)
                vt_p = jnp.concatenate([vt_p, ones_rows], axis=2)
            lam_init = 0.8 - 0.6 * math.exp(-0.3 * i)
            lam_vecs = jnp.pad(
                jnp.stack([attn_lambda_q1[j], attn_lambda_k1[j], attn_lambda_q2[j], attn_lambda_k2[j]]).astype(F32),
                ((0, V7X_SUBLANES - 4), (0, V7X_LANES - ATTN_HEAD_DIM)))
            g_pre = _row(attn_norm_pre[j])
            g_post = _row(attn_norm_post[j])
            subln = _row(attn_subln[j])

            qp = attn_q_proj(hp, g_pre, w_q_b, j, tab_p, tm=tm_p, n_pos_blocks=seqlen // tm_p)
            qs = attn_q_proj(hs, g_pre, w_q_b, j, tab_s, tm=tm_s, n_pos_blocks=1)
            heads = lambda a: a.reshape(n_dec, n_heads, V7X_LANES)
            mix_p, os_ = diff_attn(page_table, lam_vecs, qp, kb_p, vt_p, subln, heads(qs), heads(k_s), heads(v_s),
                                   cache_k, cache_v, bsz=bsz, seqlen=seqlen, n_heads=n_heads, lam_init=lam_init)
            mix_s, w_mix = os_.reshape(n_dec, qk_dim), w_o_b

        g1 = _row(mlp_norm_pre[i])
        g2 = _row(mlp_norm_post[i])
        hp = mixer_out_mlp(mix_p, w_mix, j, g_post, hp, g1, wu, wd, i, g2, tm=tm_mlp)
        hs = mixer_out_mlp(mix_s, w_mix, j, g_post, hs, g1, wu, wd, i, g2, tm=tm_s)

    y_prompt = hp.reshape(bsz, seqlen, d_model)
    y_sample = hs.reshape(n_dec, 1, d_model)
    p_k = k_p.reshape(bsz, seqlen, n_heads, V7X_LANES)
    p_v = v_p.reshape(bsz, seqlen, n_heads, V7X_LANES)
    s_k = k_s.reshape(n_dec, 1, n_heads, V7X_LANES)
    s_v = v_s.reshape(n_dec, 1, n_heads, V7X_LANES)
    return (y_prompt, y_sample, jnp.stack(ssm_p), jnp.stack(conv_p), p_k, p_v,
            ssm_s.astype(state_ssm.dtype), jnp.stack(conv_s), s_k, s_v)
```

```python
import functools
import math

import jax
import jax.numpy as jnp
from jax import lax
from jax.experimental import pallas as pl
from jax.experimental.pallas import tpu as pltpu

F32 = jnp.float32
BF16 = jnp.bfloat16

RMS_EPS = 1e-6
ROPE_THETA = 500000.0

V7X_LANES = 128
V7X_SUBLANES = 8
V7X_VMEM_BYTES = 64 * 1024 * 1024

SSD_HEAD_DIM = 64
SSD_N_GROUPS = 4
SSD_D_STATE = 128
SSD_D_CONV = 4
SSD_CHUNK = 128
ATTN_HEAD_DIM = 64
ROT_DIM = ATTN_HEAD_DIM // 4

NEG_BIG = -1e30


def _vmem_limit(nbytes):
    return int(min(V7X_VMEM_BYTES * 7 // 8, max(32 * 1024 * 1024, nbytes * 3 // 2)))


def _params(sem, vmem_bytes):
    return pltpu.CompilerParams(dimension_semantics=sem, vmem_limit_bytes=_vmem_limit(vmem_bytes))


def _const_spec(shape):
    nd = len(shape)
    return pl.BlockSpec(shape, lambda *_: (0,) * nd, pipeline_mode=pl.Buffered(1))


def _layer_spec(w, layer):
    _, rows, cols = w.shape
    return pl.BlockSpec((None, rows, cols), lambda *_: (layer, 0, 0), pipeline_mode=pl.Buffered(1))


def _whole_spec(shape):
    nd = len(shape)
    return pl.BlockSpec(shape, lambda *_: (0,) * nd)


def _rms(x, g):
    ms = jnp.mean(x * x, axis=-1, keepdims=True)
    return x * lax.rsqrt(ms + RMS_EPS) * g


def _silu(x):
    return x * (1.0 / (1.0 + jnp.exp(-x)))


def _softplus(x):
    return jnp.maximum(x, 0.0) + jnp.log1p(jnp.exp(-jnp.abs(x)))


def _dot(a, b):
    return jnp.dot(a, b, preferred_element_type=F32)


def _dot_nt(a, b):
    return lax.dot_general(a, b, (((1,), (1,)), ((), ())), preferred_element_type=F32)


def _rope(t, cos, sa, sb):
    up = pltpu.roll(t, V7X_LANES - ROT_DIM // 2, 1)
    dn = pltpu.roll(t, ROT_DIM // 2, 1)
    return t * cos + up * sa + dn * sb


def _inproj_kernel(x_ref, g_ref, w_ref, wdt_ref, z_ref, xbc_ref, dt_ref, *, d_inner, conv_dim):
    xn = _rms(x_ref[...], g_ref[...]).astype(BF16)
    z_ref[...] = _dot(xn, w_ref[:, 0:d_inner])
    xbc_ref[...] = _dot(xn, w_ref[:, d_inner:d_inner + conv_dim])
    dt_ref[...] = _dot(xn, wdt_ref[...])


def ssd_in_proj(x, g, w, w_dt, layer, *, d_inner, conv_dim, tm):
    m, d = x.shape
    dtw = w_dt.shape[-1]
    n = d_inner + conv_dim + dtw
    vm = 2 * tm * d * 4 + d * n * 2 + 2 * tm * n * 4
    return pl.pallas_call(
        functools.partial(_inproj_kernel, d_inner=d_inner, conv_dim=conv_dim),
        grid=(m // tm,),
        in_specs=[pl.BlockSpec((tm, d), lambda i: (i, 0)), _const_spec((1, d)),
                  _layer_spec(w, layer), _layer_spec(w_dt, layer)],
        out_specs=[pl.BlockSpec((tm, d_inner), lambda i: (i, 0)),
                   pl.BlockSpec((tm, conv_dim), lambda i: (i, 0)),
                   pl.BlockSpec((tm, dtw), lambda i: (i, 0))],
        out_shape=[jax.ShapeDtypeStruct((m, d_inner), F32),
                   jax.ShapeDtypeStruct((m, conv_dim), F32),
                   jax.ShapeDtypeStruct((m, dtw), F32)],
        compiler_params=_params(("parallel",), vm),
        name="ssd_in_proj",
    )(x, g, w, w_dt)


def _qproj_kernel(x_ref, g_ref, w_ref, cos_ref, sa_ref, sb_ref, q_ref, *, n_heads, scale):
    xn = _rms(x_ref[...], g_ref[...]).astype(BF16)
    cos, sa, sb = cos_ref[...], sa_ref[...], sb_ref[...]
    t = _dot(xn, w_ref[...])
    for h in range(n_heads):
        sl = slice(h * V7X_LANES, (h + 1) * V7X_LANES)
        q_ref[:, sl] = (_rope(t[:, sl], cos, sa, sb) * scale).astype(BF16)


def attn_q_proj(x, g, w, layer, tables, *, tm, n_pos_blocks):
    m, d = x.shape
    n = w.shape[-1]
    n_heads = n // V7X_LANES
    tab_spec = pl.BlockSpec((tm, V7X_LANES), lambda i: (i % n_pos_blocks, 0))
    vm = 2 * tm * d * 4 + d * n * 2 + 2 * tm * n * 2 + tm * n * 4 + 6 * tm * V7X_LANES * 4
    return pl.pallas_call(
        functools.partial(_qproj_kernel, n_heads=n_heads, scale=ATTN_HEAD_DIM ** -0.5 * math.log2(math.e)),
        grid=(m // tm,),
        in_specs=[pl.BlockSpec((tm, d), lambda i: (i, 0)), _const_spec((1, d)), _layer_spec(w, layer),
                  tab_spec, tab_spec, tab_spec],
        out_specs=pl.BlockSpec((tm, n), lambda i: (i, 0)),
        out_shape=jax.ShapeDtypeStruct((m, n), BF16),
        compiler_params=_params(("parallel",), vm),
        name="attn_q_proj",
    )(x, g, w, *tables)


def _kvproj_kernel(x_ref, g_ref, w_ref, cos_ref, sa_ref, sb_ref, k_ref, v_ref, kb_ref, vb_ref, *, n_heads):
    xn = _rms(x_ref[...], g_ref[...]).astype(BF16)
    cos, sa, sb = cos_ref[...], sa_ref[...], sb_ref[...]
    qk = n_heads * V7X_LANES
    t = _dot(xn, w_ref[:, :qk])
    for h in range(n_heads):
        sl = slice(h * V7X_LANES, (h + 1) * V7X_LANES)
        k = _rope(t[:, sl], cos, sa, sb)
        k_ref[:, sl] = k
        kb_ref[:, sl] = k.astype(BF16)
    v = _dot(xn, w_ref[:, qk:])
    v_ref[...] = v
    vb_ref[...] = v.astype(BF16)


def shared_kv_proj(x, g, w, tables, *, tm, n_pos_blocks):
    m, d = x.shape
    n = w.shape[-1]
    qk = n // 2
    n_heads = qk // V7X_LANES
    tab_spec = pl.BlockSpec((tm, V7X_LANES), lambda i: (i % n_pos_blocks, 0))
    row = lambda i: (i, 0)
    vm = 2 * tm * d * 4 + d * n * 2 + 2 * tm * n * 6 + tm * n * 4 + 6 * tm * V7X_LANES * 4
    return pl.pallas_call(
        functools.partial(_kvproj_kernel, n_heads=n_heads),
        grid=(m // tm,),
        in_specs=[pl.BlockSpec((tm, d), row), _const_spec((1, d)), _layer_spec(w, 0),
                  tab_spec, tab_spec, tab_spec],
        out_specs=[pl.BlockSpec((tm, qk), row)] * 4,
        out_shape=[jax.ShapeDtypeStruct((m, qk), F32), jax.ShapeDtypeStruct((m, qk), F32),
                   jax.ShapeDtypeStruct((m, qk), BF16), jax.ShapeDtypeStruct((m, qk), BF16)],
        compiler_params=_params(("parallel",), vm),
        name="shared_kv_proj",
    )(x, g, w, *tables)


def _mixer_out_mlp_kernel(a_ref, wo_ref, go_ref, h_ref, g1_ref, wu_ref, wd_ref, g2_ref, o_ref, *, tf):
    y = _dot(a_ref[...].astype(BF16), wo_ref[...])
    h = h_ref[...] + _rms(y, go_ref[...])
    xn = _rms(h, g1_ref[...]).astype(BF16)
    d_ff = wu_ref.shape[1]
    acc = jnp.zeros(h.shape, F32)
    for f in range(d_ff // tf):
        u = jnp.maximum(_dot(xn, wu_ref[:, f * tf:(f + 1) * tf]), 0.0)
        acc = acc + _dot((u * u).astype(BF16), wd_ref[f * tf:(f + 1) * tf, :])
    o_ref[...] = h + _rms(acc, g2_ref[...])


def mixer_out_mlp(a, w_o, o_layer, g_o, h, g_pre, w_up, w_down, layer, g_post, *, tm, tf=512):
    m, k = a.shape
    d = h.shape[1]
    d_ff = w_up.shape[-1]
    row = lambda i: (i, 0)
    vm = (2 * tm * k * a.dtype.itemsize + k * d * 2 + 4 * tm * d * 4 + 2 * d * d_ff * 2
          + 3 * tm * d * 4 + 2 * tm * tf * 4)
    return pl.pallas_call(
        functools.partial(_mixer_out_mlp_kernel, tf=tf),
        grid=(m // tm,),
        in_specs=[pl.BlockSpec((tm, k), row), _layer_spec(w_o, o_layer), _const_spec((1, d)),
                  pl.BlockSpec((tm, d), row), _const_spec((1, d)), _layer_spec(w_up, layer),
                  _layer_spec(w_down, layer), _const_spec((1, d))],
        out_specs=pl.BlockSpec((tm, d), row),
        out_shape=jax.ShapeDtypeStruct((m, d), F32),
        compiler_params=_params(("parallel",), vm),
        name="mixer_out_mlp",
    )(a, w_o, g_o, h, g_pre, w_up, w_down, g_post)


def _expand3(v, e):
    hi = v.astype(BF16)
    r1 = v - hi.astype(F32)
    mid = r1.astype(BF16)
    lo = (r1 - mid.astype(F32)).astype(BF16)
    return _dot(hi, e) + _dot(mid, e) + _dot(lo, e)


def _gated_group_norm(y, z, gate_norm, n_groups):
    g = y * _silu(z)
    width = g.shape[1] // n_groups
    outs = []
    for i in range(n_groups):
        gg = g[:, i * width:(i + 1) * width]
        ms = jnp.mean(gg * gg, axis=-1, keepdims=True)
        outs.append(gg * lax.rsqrt(ms + RMS_EPS) * gate_norm[:, i * width:(i + 1) * width])
    return outs


def _ssd_prompt_kernel(z_ref, xbc_ref, dtr_ref, cw_ref, cb_ref, dtb_ref, alog_ref, dexp_ref, gn_ref,
                       tri_ref, e64_ref,
                       g_ref, ssm_ref, conv_ref,
                       cbuf, st_t, y_s, *, n_heads, d_inner):
    L = SSD_CHUNK
    N = SSD_D_STATE
    gn_w = SSD_N_GROUPS * N
    heads_per_group = n_heads // SSD_N_GROUPS
    c = pl.program_id(1)
    last = pl.num_programs(1) - 1
    halo = V7X_SUBLANES

    @pl.when(c == 0)
    def _():
        cbuf[0:halo, :] = jnp.zeros((halo, cbuf.shape[1]), F32)
        st_t[...] = jnp.zeros(st_t.shape, F32)

    cbuf[halo:halo + L, :] = xbc_ref[...]
    xw = cbuf[...]
    acc = cb_ref[...] + xw[halo:, :] * cw_ref[SSD_D_CONV - 1:SSD_D_CONV, :]
    for k in range(1, SSD_D_CONV):
        acc = acc + pltpu.roll(xw, k, 0)[halo:, :] * cw_ref[SSD_D_CONV - 1 - k:SSD_D_CONV - k, :]
    act = _silu(acc)
    x = act[:, :d_inner]
    cbuf[0:halo, :] = xw[L:, :]

    dt = _softplus(dtr_ref[...] + dtb_ref[...])
    a = -jnp.exp(alog_ref[...]) * math.log2(math.e)
    acum = jnp.dot(tri_ref[...], a * dt, precision=lax.Precision.HIGHEST, preferred_element_type=F32)
    acum_last = acum[L - 1:L, :]
    acum_t = acum.T
    cd = _expand3(jnp.broadcast_to(jnp.exp2(acum_last), (V7X_SUBLANES, V7X_LANES)), e64_ref[...])[0:1, :]
    e64 = e64_ref[...]
    dt_x = _dot(dt.astype(BF16), e64)
    w_x = _dot((jnp.exp2(acum_last - acum) * dt).astype(BF16), e64)
    e1_x = _dot(jnp.exp2(acum).astype(BF16), e64)
    xdt = (x * dt_x).astype(BF16)
    xs = (x * w_x).astype(BF16)

    li = lax.broadcasted_iota(jnp.int32, (L, L), 0)
    si = lax.broadcasted_iota(jnp.int32, (L, L), 1)
    causal = li >= si
    lo_half = lax.broadcasted_iota(jnp.int32, (L, V7X_LANES), 1) < SSD_HEAD_DIM

    for grp in range(SSD_N_GROUPS):
        b_g = act[:, d_inner + grp * N:d_inner + (grp + 1) * N]
        c_g = act[:, d_inner + gn_w + grp * N:d_inner + gn_w + (grp + 1) * N].astype(BF16)
        cb_g = jnp.where(causal, _dot_nt(c_g, b_g.astype(BF16)), 0.0)
        bt_g = b_g.T.astype(BF16)
        for pr in range(heads_per_group // 2):
            q = grp * (heads_per_group // 2) + pr
            sl = slice(q * V7X_LANES, (q + 1) * V7X_LANES)
            st_prev = st_t[:, sl]
            y_off = _dot(c_g, st_prev.astype(BF16)) * e1_x[:, sl]
            st_t[:, sl] = st_prev * cd[:, sl] + _dot(bt_g, xs[:, sl])
            xp = xdt[:, sl]
            ys = []
            for h in (2 * q, 2 * q + 1):
                seg = jnp.minimum(jnp.broadcast_to(acum[:, h:h + 1], (L, L)) - acum_t[h:h + 1, :], 0.0)
                ys.append(_dot((jnp.exp2(seg) * cb_g).astype(BF16), xp))
            y_s[:, sl] = jnp.where(lo_half, ys[0], ys[1]) + y_off

    y = y_s[...] + x * dexp_ref[...]
    outs = _gated_group_norm(y, z_ref[...], gn_ref[...], SSD_N_GROUPS)
    width = d_inner // SSD_N_GROUPS
    for i, o in enumerate(outs):
        g_ref[:, i * width:(i + 1) * width] = o.astype(BF16)

    @pl.when(c == last)
    def _():
        conv_ref[...] = cbuf[0:halo, :]
        for q in range(n_heads // 2):
            sl = slice(q * V7X_LANES, (q + 1) * V7X_LANES)
            ssm_ref[sl, :] = st_t[:, sl].T


def ssd_prompt(z, xbc, dtr, cw, cb, dtb, alog, dexp, gn, consts, *, bsz, seqlen, n_heads):
    m, d_inner = z.shape
    conv_dim = xbc.shape[1]
    L = SSD_CHUNK
    nc = seqlen // L
    tri, e64 = consts
    blk = lambda b, c: (b * nc + c, 0)
    vm = (2 * L * (d_inner + conv_dim + V7X_LANES) * 4 + 2 * L * d_inner * 2
          + (L + 8) * conv_dim * 4 + 2 * L * d_inner * 4
          + 2 * e64.size * 2 + 4 * n_heads * SSD_HEAD_DIM * SSD_D_STATE * 4
          + 8 * L * conv_dim * 4)
    return pl.pallas_call(
        functools.partial(_ssd_prompt_kernel, n_heads=n_heads, d_inner=d_inner),
        grid=(bsz, nc),
        in_specs=[pl.BlockSpec((L, d_inner), blk), pl.BlockSpec((L, conv_dim), blk),
                  pl.BlockSpec((L, V7X_LANES), blk),
                  _const_spec(cw.shape), _const_spec(cb.shape), _const_spec(dtb.shape),
                  _const_spec(alog.shape), _const_spec(dexp.shape), _const_spec(gn.shape),
                  _const_spec(tri.shape), _const_spec(e64.shape)],
        out_specs=[pl.BlockSpec((L, d_inner), blk),
                   pl.BlockSpec((None, n_heads * SSD_HEAD_DIM, SSD_D_STATE), lambda b, c: (b, 0, 0)),
                   pl.BlockSpec((None, V7X_SUBLANES, conv_dim), lambda b, c: (b, 0, 0))],
        out_shape=[jax.ShapeDtypeStruct((m, d_inner), BF16),
                   jax.ShapeDtypeStruct((bsz, n_heads * SSD_HEAD_DIM, SSD_D_STATE), F32),
                   jax.ShapeDtypeStruct((bsz, V7X_SUBLANES, conv_dim), F32)],
        scratch_shapes=[pltpu.VMEM((L + V7X_SUBLANES, conv_dim), F32),
                        pltpu.VMEM((SSD_D_STATE, d_inner), F32),
                        pltpu.VMEM((L, d_inner), F32)],
        compiler_params=_params(("arbitrary", "arbitrary"), vm),
        name="ssd_prompt",
    )(z, xbc, dtr, cw, cb, dtb, alog, dexp, gn, tri, e64)


def _ssd_sample_conv_kernel(xbc_ref, cs_ref, dtr_ref, cw_ref, cb_ref, dtb_ref,
                            x_ref, b_ref, c_ref, dt_ref, ncs_ref, *, d_inner):
    new = xbc_ref[...]
    acc = cb_ref[...] + new * cw_ref[SSD_D_CONV - 1:SSD_D_CONV, :]
    for k in range(SSD_D_CONV - 1):
        acc = acc + cs_ref[k] * cw_ref[k:k + 1, :]
    act = _silu(acc)
    gn_w = SSD_N_GROUPS * SSD_D_STATE
    x_ref[...] = act[:, :d_inner]
    b_ref[...] = act[:, d_inner:d_inner + gn_w]
    c_ref[...] = act[:, d_inner + gn_w:]
    dt_ref[...] = _softplus(dtr_ref[...] + dtb_ref[...])
    for k in range(SSD_D_CONV - 2):
        ncs_ref[k] = cs_ref[k + 1]
    ncs_ref[SSD_D_CONV - 2] = new


def ssd_sample_conv(xbc, cs_t, dtr, cw, cb, dtb, *, d_inner):
    n, conv_dim = xbc.shape
    gn_w = SSD_N_GROUPS * SSD_D_STATE
    args = (xbc, cs_t, dtr, cw, cb, dtb)
    vm = 6 * xbc.size * 4 + 4 * cs_t.size * 4
    return pl.pallas_call(
        functools.partial(_ssd_sample_conv_kernel, d_inner=d_inner),
        grid=(1,),
        in_specs=[_whole_spec(a.shape) for a in args],
        out_specs=[_whole_spec((n, d_inner)), _whole_spec((n, gn_w)), _whole_spec((n, gn_w)),
                   _whole_spec(dtr.shape), _whole_spec(cs_t.shape)],
        out_shape=[jax.ShapeDtypeStruct((n, d_inner), F32), jax.ShapeDtypeStruct((n, gn_w), F32),
                   jax.ShapeDtypeStruct((n, gn_w), F32), jax.ShapeDtypeStruct(dtr.shape, F32),
                   jax.ShapeDtypeStruct(cs_t.shape, F32)],
        compiler_params=_params(("arbitrary",), vm),
        name="ssd_sample_conv",
    )(*args)


def _ssd_sample_state_kernel(st_ref, xt_ref, dtb_ref, dtr_ref, alog_ref, b_ref, c_ref, *rest, sb, n_heads):
    nst_ref, y_ref = rest[-2:]
    heads_per_group = n_heads // SSD_N_GROUPS
    N = SSD_D_STATE
    P = SSD_HEAD_DIM
    a_rep = -jnp.exp(alog_ref[...])
    for s in range(sb):
        dec = jnp.exp(a_rep * dtb_ref[s])
        xdt_t = xt_ref[s] * dtr_ref[s]
        for grp in range(SSD_N_GROUPS):
            b_row = b_ref[s][:, grp * N:(grp + 1) * N]
            c_row = c_ref[s][:, grp * N:(grp + 1) * N]
            news = []
            for r in range(heads_per_group):
                h = grp * heads_per_group + r
                new = st_ref[s, h] * dec[h:h + 1, :] + xdt_t[:, h:h + 1] * b_row
                nst_ref[s, h] = new
                news.append(new)
            hg = jnp.concatenate(news, axis=0).astype(BF16)
            c8 = jnp.broadcast_to(c_row, (V7X_SUBLANES, N)).astype(BF16)
            yg = _dot_nt(c8, hg)
            y_ref[s, :, grp * heads_per_group * P:(grp + 1) * heads_per_group * P] = yg[0:1, :]


def ssd_sample_state(states, layer, stacked, x_t, dt_bcast, dt_row, alog_rep, b_in, c_in, *, sb):
    _, n, n_heads, P, N = states.shape
    d_inner = n_heads * P
    gn_w = b_in.shape[-1]
    lyr = lambda i: (layer, i, 0, 0, 0)
    blk3 = lambda i: (i, 0, 0)
    vm = 4 * sb * n_heads * P * N * 4 + 4 * sb * (P * V7X_LANES + n_heads * V7X_LANES) * 4
    args = [states, x_t, dt_bcast, dt_row, alog_rep, b_in, c_in]
    in_specs = [pl.BlockSpec((None, sb, n_heads, P, N), lyr),
                pl.BlockSpec((sb, P, n_heads), blk3),
                pl.BlockSpec((sb, n_heads, V7X_LANES), blk3),
                pl.BlockSpec((sb, 1, n_heads), blk3),
                _const_spec(alog_rep.shape),
                pl.BlockSpec((sb, 1, gn_w), blk3),
                pl.BlockSpec((sb, 1, gn_w), blk3)]
    aliases = {}
    if stacked is not None:
        aliases = {len(args): 0}
        args.append(stacked)
        in_specs.append(pl.BlockSpec(memory_space=pl.ANY))
    return pl.pallas_call(
        functools.partial(_ssd_sample_state_kernel, sb=sb, n_heads=n_heads),
        grid=(n // sb,),
        in_specs=in_specs,
        out_specs=[pl.BlockSpec((None, sb, n_heads, P, N), lyr),
                   pl.BlockSpec((sb, 1, d_inner), blk3)],
        out_shape=[jax.ShapeDtypeStruct(states.shape, F32),
                   jax.ShapeDtypeStruct((n, 1, d_inner), F32)],
        input_output_aliases=aliases,
        compiler_params=_params(("parallel",), vm),
        name="ssd_sample_state",
    )(*args)


def _ssd_sample_gate_kernel(y_ref, x_ref, z_ref, dexp_ref, gn_ref, g_ref):
    y = y_ref[...] + x_ref[...] * dexp_ref[...]
    outs = _gated_group_norm(y, z_ref[...], gn_ref[...], SSD_N_GROUPS)
    width = y.shape[1] // SSD_N_GROUPS
    for i, o in enumerate(outs):
        g_ref[:, i * width:(i + 1) * width] = o.astype(BF16)


def ssd_sample_gate(y, x, z, dexp, gn):
    args = (y, x, z, dexp, gn)
    return pl.pallas_call(
        _ssd_sample_gate_kernel,
        grid=(1,),
        in_specs=[_whole_spec(a.shape) for a in args],
        out_specs=_whole_spec(y.shape),
        out_shape=jax.ShapeDtypeStruct(y.shape, BF16),
        compiler_params=_params(("arbitrary",), 8 * y.size * 4),
        name="ssd_sample_gate",
    )(*args)


def _diff_lambda(lam_ref, lam_init):
    v = lam_ref[...]
    d1 = jnp.sum(v[0:1, :] * v[1:2, :], axis=-1, keepdims=True)
    d2 = jnp.sum(v[2:3, :] * v[3:4, :], axis=-1, keepdims=True)
    return jnp.exp(d1) - jnp.exp(d2) + lam_init


V_ROWS_PAD = 16


def _flash_body(lam_ref, q_ref, k_ref, vt_ref, subln_ref, o_ref, scratch, *, t, hb, unroll, lam_init, tail_work):
    qi = pl.program_id(2)
    m_s, acc_s = scratch[:hb], scratch[hb:]
    q2 = []
    for s in range(hb):
        q = q_ref[:, s * V7X_LANES:(s + 1) * V7X_LANES]
        lane = lax.broadcasted_iota(jnp.int32, q.shape, 1)
        zero = jnp.zeros_like(q)
        q2.append(jnp.concatenate([jnp.where(lane < ATTN_HEAD_DIM, q, zero),
                                   jnp.where(lane >= ATTN_HEAD_DIM, q, zero)], axis=0))
        m_s[s][...] = jnp.full(m_s[s].shape, NEG_BIG, F32)
        acc_s[s][...] = jnp.zeros(acc_s[s].shape, F32)

    def accumulate(j, scs, masked):
        start = pl.multiple_of(j * t, t)
        for s in range(hb):
            sc = scs[s]
            if masked:
                kv = lax.broadcasted_iota(jnp.int32, sc.shape, 0)
                qq = lax.broadcasted_iota(jnp.int32, sc.shape, 1)
                qq = jnp.where(qq >= t, qq - t, qq)
                sc = jnp.where(kv <= qq, sc, NEG_BIG)
            m_old = m_s[s][...]
            m_new = jnp.maximum(m_old, jnp.max(sc, axis=0, keepdims=True))
            alpha = jnp.exp2(m_old - m_new)
            p = jnp.exp2(sc - m_new).astype(BF16)
            acc_s[s][...] = alpha * acc_s[s][...] + _dot(vt_ref[s, :, pl.ds(start, t)], p)
            m_s[s][...] = m_new

    def run_blocks(j0, n, masked_last):
        start = pl.multiple_of(j0 * t, t)
        wide = [_dot_nt(k_ref[pl.ds(start, n * t), s * V7X_LANES:(s + 1) * V7X_LANES], q2[s])
                for s in range(hb)]
        for u in range(n):
            accumulate(j0 + u, [w[u * t:(u + 1) * t, :] for w in wide], masked_last and u == n - 1)

    def body(g, carry):
        run_blocks(g * unroll, unroll, False)
        return carry

    n_groups = qi // unroll
    lax.fori_loop(0, n_groups, body, 0)
    rem = qi - n_groups * unroll
    for r in range(unroll):
        @pl.when(rem == r)
        def _(r=r):
            run_blocks(n_groups * unroll, r + 1, True)
            tail_work()

    lam = _diff_lambda(lam_ref, lam_init)
    gain = subln_ref[...] * (1.0 - lam_init)
    for s in range(hb):
        acc = acc_s[s][...]
        acc = acc[:V7X_LANES, :] * (1.0 / acc[V7X_LANES:V7X_LANES + 1, :])
        o_t = acc[:, :t] - lam * acc[:, t:]
        ms = jnp.mean(o_t * o_t, axis=0, keepdims=True)
        o_t = o_t * lax.rsqrt(ms + RMS_EPS)
        o_ref[:, s * V7X_LANES:(s + 1) * V7X_LANES] = (o_t.T * gain).astype(BF16)


def _decode_body(lam_ref, q_ref, kn_ref, vn_ref, subln_ref, k_refs, v_refs, o_ref, s, *, lam_init):
    n_pages = len(k_refs)
    _, page, n_heads, hd = k_refs[0].shape
    rows = 2 * n_heads
    prow = page * n_heads
    q8 = q_ref[s].astype(F32)
    lane = lax.broadcasted_iota(jnp.int32, q8.shape, 1)
    q16 = jnp.concatenate([jnp.where(lane < ATTN_HEAD_DIM, q8, 0.0),
                           jnp.where(lane >= ATTN_HEAD_DIM, q8, 0.0)], axis=0)
    q16b = q16.astype(BF16)
    r_i = lax.broadcasted_iota(jnp.int32, (rows, prow), 0)
    c_i = lax.broadcasted_iota(jnp.int32, (rows, prow), 1)
    own = (r_i % n_heads) == (c_i % n_heads)

    s_parts = []
    for p in range(n_pages):
        kp = k_refs[p][0].reshape(prow, hd).astype(BF16)
        s_parts.append(jnp.where(own, _dot_nt(q16b, kp), NEG_BIG))
    kn = kn_ref[s]
    kn2 = jnp.concatenate([kn, kn], axis=0)
    s_new = jnp.sum(q16 * kn2, axis=1, keepdims=True)
    m = s_new
    for sp in s_parts:
        m = jnp.maximum(m, jnp.max(sp, axis=1, keepdims=True))
    p_new = jnp.exp2(s_new - m)
    vn = vn_ref[s]
    acc = p_new * jnp.concatenate([vn, vn], axis=0)
    denom = p_new
    for p in range(n_pages):
        pp = jnp.exp2(s_parts[p] - m)
        denom = denom + jnp.sum(pp, axis=1, keepdims=True)
        acc = acc + _dot(pp.astype(BF16), v_refs[p][0].reshape(prow, hd).astype(BF16))

    lam = _diff_lambda(lam_ref, lam_init)
    acc = acc * (1.0 / denom)
    d = acc[:n_heads, :] - lam * acc[n_heads:, :]
    ms = jnp.mean(d * d, axis=1, keepdims=True)
    o_ref[s] = d * lax.rsqrt(ms + RMS_EPS) * (subln_ref[...] * (1.0 - lam_init))


def _attn_kernel(pt_ref, lam_ref, q_ref, k_ref, vt_ref, subln_ref, qd_ref, kn_ref, vn_ref, *rest,
                 n_pages, spp, t, hb, unroll, lam_init):
    del pt_ref
    n_pg = spp * n_pages
    k_refs, v_refs = rest[:n_pg], rest[n_pg:2 * n_pg]
    o_ref, od_ref = rest[2 * n_pg], rest[2 * n_pg + 1]
    scratch = rest[2 * n_pg + 2:]
    def decode():
        for s in range(spp):
            _decode_body(lam_ref, qd_ref, kn_ref, vn_ref, subln_ref,
                         k_refs[s * n_pages:(s + 1) * n_pages], v_refs[s * n_pages:(s + 1) * n_pages],
                         od_ref, s, lam_init=lam_init)

    _flash_body(lam_ref, q_ref, k_ref, vt_ref, subln_ref, o_ref, scratch,
                t=t, hb=hb, unroll=unroll, lam_init=lam_init, tail_work=decode)


def diff_attn(page_table, lam_vecs, q, k, v_t, subln, q_dec, k_new, v_new, cache_k, cache_v, *,
              bsz, seqlen, n_heads, lam_init, t=256, hb=4, unroll=2):
    nq = seqlen // t
    n_hg = n_heads // hb
    n_steps = bsz * n_hg * nq
    n, n_pages = page_table.shape
    spp, ragged = divmod(n, n_steps)
    assert spp >= 1 and ragged == 0, "sample sequences must tile the prompt attention grid"
    _, page, _, hd = cache_k.shape
    vr = v_t.shape[2]
    w = hb * V7X_LANES
    step = lambda b, h, i: (b * n_hg + h) * nq + i
    qblk = lambda b, h, i, pt: (b * nq + i, h)
    dec = lambda b, h, i, pt: (step(b, h, i), 0, 0)
    const2 = lambda b, h, i, pt: (0, 0)
    page_specs = [pl.BlockSpec((1, page, n_heads, hd),
                               functools.partial(lambda b, h, i, pt, s, p: (pt[step(b, h, i) * spp + s, p], 0, 0, 0),
                                                 s=s, p=p))
                  for s in range(spp) for p in range(n_pages)]
    resident = dict(pipeline_mode=pl.Buffered(1))
    vm = (2 * 2 * t * w * 2 + seqlen * w * 2 + hb * vr * seqlen * 2 + hb * (vr + 8) * 2 * t * 4
          + hb * 4 * t * 2 * t * 4
          + 2 * 2 * spp * n_pages * page * n_heads * hd * 4 + 4 * 2 * n_heads * page * n_heads * n_pages * 4)
    grid_spec = pltpu.PrefetchScalarGridSpec(
        num_scalar_prefetch=1,
        grid=(bsz, n_hg, nq),
        in_specs=[pl.BlockSpec(lam_vecs.shape, const2),
                  pl.BlockSpec((t, w), qblk),
                  pl.BlockSpec((seqlen, w), lambda b, h, i, pt: (b, h), **resident),
                  pl.BlockSpec((None, hb, vr, seqlen), lambda b, h, i, pt: (b, h, 0, 0), **resident),
                  pl.BlockSpec(subln.shape, const2),
                  pl.BlockSpec((spp, n_heads, hd), dec), pl.BlockSpec((spp, n_heads, hd), dec),
                  pl.BlockSpec((spp, n_heads, hd), dec)]
                 + page_specs + page_specs,
        out_specs=[pl.BlockSpec((t, w), qblk), pl.BlockSpec((spp, n_heads, hd), dec)],
        scratch_shapes=[pltpu.VMEM((1, 2 * t), F32)] * hb + [pltpu.VMEM((vr, 2 * t), F32)] * hb,
    )
    return pl.pallas_call(
        functools.partial(_attn_kernel, n_pages=n_pages, spp=spp, t=t, hb=hb, unroll=unroll, lam_init=lam_init),
        grid_spec=grid_spec,
        out_shape=[jax.ShapeDtypeStruct(q.shape, BF16), jax.ShapeDtypeStruct((n, n_heads, hd), F32)],
        compiler_params=_params(("parallel", "parallel", "arbitrary"), vm),
        name="diff_attn",
    )(page_table, lam_vecs, q, k, v_t, subln, q_dec, k_new, v_new,
      *([cache_k] * (spp * n_pages)), *([cache_v] * (spp * n_pages)))


def _rope_tables(pos):
    half = ROT_DIM // 2
    inv = ROPE_THETA ** (-jnp.arange(0, ROT_DIM, 2, dtype=F32) / ROT_DIM)
    ang = pos.astype(F32)[:, None] * inv[None, :]
    cos, sin = jnp.cos(ang), jnp.sin(ang)
    n = pos.shape[0]
    pad = jnp.zeros((n, ATTN_HEAD_DIM - ROT_DIM), F32)
    zeros = jnp.zeros((n, half), F32)
    c_map = jnp.concatenate([cos, cos, pad + 1.0], axis=1)
    sa_map = jnp.concatenate([-sin, zeros, pad], axis=1)
    sb_map = jnp.concatenate([zeros, sin, pad], axis=1)
    return tuple(jnp.concatenate([t, t], axis=1) for t in (c_map, sa_map, sb_map))


def _row(v, width=None):
    v = v.astype(F32).reshape(1, -1)
    if width is not None and v.shape[1] < width:
        v = jnp.pad(v, ((0, 0), (0, width - v.shape[1])))
    return v


def kernel(x_prompt, x_sample, state_ssm, state_conv, cache_k, cache_v, page_table, ssd_norm_pre, ssd_norm_post, ssd_w_in, ssd_conv_w, ssd_conv_b, ssd_dt_bias, ssd_a_log, ssd_d, ssd_gate_norm, ssd_w_out, mlp_norm_pre, mlp_norm_post, mlp_w_up, mlp_w_down, kv_norm, w_kv, attn_norm_pre, attn_norm_post, attn_w_q, attn_lambda_q1, attn_lambda_k1, attn_lambda_q2, attn_lambda_k2, attn_subln, attn_w_o):
    bsz, seqlen, d_model = x_prompt.shape
    n_dec = x_sample.shape[0]
    n_ssd = ssd_w_in.shape[0]
    depth = mlp_w_up.shape[0]
    n_heads_ssd = ssd_a_log.shape[1]
    d_inner = n_heads_ssd * SSD_HEAD_DIM
    conv_dim = ssd_conv_w.shape[2]
    qk_dim = attn_w_q.shape[2]
    n_heads = qk_dim // V7X_LANES
    m_p = bsz * seqlen

    hp = x_prompt.reshape(m_p, d_model)
    hs = x_sample.reshape(n_dec, d_model)

    tm_p = 512
    tm_mlp = 512
    tm_s = n_dec

    L = SSD_CHUNK
    tri = (jnp.arange(L)[:, None] >= jnp.arange(L)[None, :]).astype(F32)
    hrow = jnp.arange(V7X_LANES)[:, None]
    e64 = (hrow == (jnp.arange(d_inner)[None, :] // SSD_HEAD_DIM)).astype(BF16)

    tab_p = _rope_tables(jnp.arange(seqlen))
    past_len = page_table.shape[1] * cache_k.shape[1]
    tab_s = tuple(jnp.broadcast_to(t, (n_dec, V7X_LANES)) for t in _rope_tables(jnp.full((1,), past_len)))

    ssm_p, conv_p, conv_s = [], [], []
    ssm_s = None
    k_p = v_p = k_s = v_s = None
    kb_p = vt_p = None

    w_in_b = ssd_w_in.astype(BF16)
    w_dt_b = jnp.pad(w_in_b[:, :, d_inner + conv_dim:],
                     ((0, 0), (0, 0), (0, V7X_LANES - n_heads_ssd)))
    w_out_b = ssd_w_out.astype(BF16)
    w_q_b = attn_w_q.astype(BF16)
    w_o_b = attn_w_o.astype(BF16)
    w_kv_b = w_kv.astype(BF16)[None]
    wu = mlp_w_up.astype(BF16)
    wd = mlp_w_down.astype(BF16)

    for i in range(depth):
        if i < n_ssd:
            j = i
            g_pre = _row(ssd_norm_pre[j])
            g_post = _row(ssd_norm_post[j])
            cw = ssd_conv_w[j].astype(F32)
            cb = _row(ssd_conv_b[j])
            dtb = _row(ssd_dt_bias[j], V7X_LANES)
            alog = _row(ssd_a_log[j], V7X_LANES)
            dexp = jnp.repeat(ssd_d[j].astype(F32), SSD_HEAD_DIM).reshape(1, d_inner)
            gn = _row(ssd_gate_norm[j])

            z, xbc, dtr = ssd_in_proj(hp, g_pre, w_in_b, w_dt_b, j, d_inner=d_inner, conv_dim=conv_dim, tm=tm_p)
            g, sp, cp = ssd_prompt(z, xbc, dtr, cw, cb, dtb, alog, dexp, gn, (tri, e64),
                                   bsz=bsz, seqlen=seqlen, n_heads=n_heads_ssd)
            mix_p, mix_s, w_mix = g, None, w_out_b
            ssm_p.append(sp.reshape(bsz, n_heads_ssd, SSD_HEAD_DIM, SSD_D_STATE))
            conv_p.append(cp[:, V7X_SUBLANES - (SSD_D_CONV - 1):, :])

            zs, xbcs, dtrs = ssd_in_proj(hs, g_pre, w_in_b, w_dt_b, j, d_inner=d_inner, conv_dim=conv_dim, tm=tm_s)
            cs_t = jnp.transpose(state_conv[j].astype(F32), (1, 0, 2))
            xs, bs_, cs_, dts, ncs = ssd_sample_conv(xbcs, cs_t, dtrs, cw, cb, dtb, d_inner=d_inner)
            x_t = jnp.transpose(xs.reshape(n_dec, n_heads_ssd, SSD_HEAD_DIM), (0, 2, 1))
            dt_h = dts[:, :n_heads_ssd]
            dt_bcast = jnp.broadcast_to(dt_h[:, :, None], (n_dec, n_heads_ssd, V7X_LANES))
            alog_rep = jnp.broadcast_to(ssd_a_log[j].astype(F32)[:, None], (n_heads_ssd, V7X_LANES))
            ssm_s, ys = ssd_sample_state(state_ssm, j, ssm_s, x_t, dt_bcast, dt_h[:, None, :], alog_rep,
                                         bs_[:, None, :], cs_[:, None, :], sb=4)
            mix_s = ssd_sample_gate(ys.reshape(n_dec, d_inner), xs, zs, dexp, gn)
            conv_s.append(jnp.transpose(ncs, (1, 0, 2)).astype(state_conv.dtype))
        else:
            j = i - n_ssd
            if j == 0:
                g_kv = _row(kv_norm)
                k_p, v_p, kb_p, vb_p = shared_kv_proj(hp, g_kv, w_kv_b, tab_p, tm=tm_p, n_pos_blocks=seqlen // tm_p)
                k_s, v_s, _, _ = shared_kv_proj(hs, g_kv, w_kv_b, tab_s, tm=tm_s, n_pos_blocks=1)
                vt_p = jnp.transpose(vb_p.reshape(bsz, seqlen, n_heads, V7X_LANES), (0, 2, 3, 1))
                ones_rows = jnp.zeros((bsz, n_heads, V_ROWS_PAD, seqlen), BF16).at[:, :, 0, :].set(1.0)
                vt_p = jnp.concatenate([vt_p, ones_rows], axis=2)
            lam_init = 0.8 - 0.6 * math.exp(-0.3 * i)
            lam_vecs = jnp.pad(
                jnp.stack([attn_lambda_q1[j], attn_lambda_k1[j], attn_lambda_q2[j], attn_lambda_k2[j]]).astype(F32),
                ((0, V7X_SUBLANES - 4), (0, V7X_LANES - ATTN_HEAD_DIM)))
            g_pre = _row(attn_norm_pre[j])
            g_post = _row(attn_norm_post[j])
            subln = _row(attn_subln[j])

            qp = attn_q_proj(hp, g_pre, w_q_b, j, tab_p, tm=tm_p, n_pos_blocks=seqlen // tm_p)
            qs = attn_q_proj(hs, g_pre, w_q_b, j, tab_s, tm=tm_s, n_pos_blocks=1)
            heads = lambda a: a.reshape(n_dec, n_heads, V7X_LANES)
            mix_p, os_ = diff_attn(page_table, lam_vecs, qp, kb_p, vt_p, subln, heads(qs), heads(k_s), heads(v_s),
                                   cache_k, cache_v, bsz=bsz, seqlen=seqlen, n_heads=n_heads, lam_init=lam_init)
            mix_s, w_mix = os_.reshape(n_dec, qk_dim), w_o_b

        g1 = _row(mlp_norm_pre[i])
        g2 = _row(mlp_norm_post[i])
        hp = mixer_out_mlp(mix_p, w_mix, j, g_post, hp, g1, wu, wd, i, g2, tm=tm_mlp)
        hs = mixer_out_mlp(mix_s, w_mix, j, g_post, hs, g1, wu, wd, i, g2, tm=tm_s)

    y_prompt = hp.reshape(bsz, seqlen, d_model)
    y_sample = hs.reshape(n_dec, 1, d_model)
    p_k = k_p.reshape(bsz, seqlen, n_heads, V7X_LANES)
    p_v = v_p.reshape(bsz, seqlen, n_heads, V7X_LANES)
    s_k = k_s.reshape(n_dec, 1, n_heads, V7X_LANES)
    s_v = v_s.reshape(n_dec, 1, n_heads, V7X_LANES)
    return (y_prompt, y_sample, jnp.stack(ssm_p), jnp.stack(conv_p), p_k, p_v,
            ssm_s.astype(state_ssm.dtype), jnp.stack(conv_s), s_k, s_v)
```

```python
import functools
import math

import jax
import jax.numpy as jnp
from jax import lax
from jax.experimental import pallas as pl
from jax.experimental.pallas import tpu as pltpu

F32 = jnp.float32
BF16 = jnp.bfloat16

RMS_EPS = 1e-6
ROPE_THETA = 500000.0

V7X_LANES = 128
V7X_SUBLANES = 8
V7X_VMEM_BYTES = 64 * 1024 * 1024

SSD_HEAD_DIM = 64
SSD_N_GROUPS = 4
SSD_D_STATE = 128
SSD_D_CONV = 4
SSD_CHUNK = 128
ATTN_HEAD_DIM = 64
ROT_DIM = ATTN_HEAD_DIM // 4

NEG_BIG = -1e30
V_ROWS_PAD = 2 * V7X_SUBLANES


def _vmem_limit(nbytes):
    return int(min(V7X_VMEM_BYTES * 7 // 8, max(32 * 1024 * 1024, nbytes * 3 // 2)))


def _params(sem, vmem_bytes):
    return pltpu.CompilerParams(dimension_semantics=sem, vmem_limit_bytes=_vmem_limit(vmem_bytes))


def _const_spec(shape):
    nd = len(shape)
    return pl.BlockSpec(shape, lambda *_: (0,) * nd, pipeline_mode=pl.Buffered(1))


def _layer_spec(w, layer):
    _, rows, cols = w.shape
    return pl.BlockSpec((None, rows, cols), lambda *_: (layer, 0, 0), pipeline_mode=pl.Buffered(1))


def _whole_spec(shape):
    nd = len(shape)
    return pl.BlockSpec(shape, lambda *_: (0,) * nd)


def _rms(x, g):
    ms = jnp.mean(x * x, axis=-1, keepdims=True)
    return x * lax.rsqrt(ms + RMS_EPS) * g


def _silu(x):
    return x * (1.0 / (1.0 + jnp.exp(-x)))


def _softplus(x):
    return jnp.maximum(x, 0.0) + jnp.log1p(jnp.exp(-jnp.abs(x)))


def _dot(a, b):
    return jnp.dot(a, b, preferred_element_type=F32)


def _dot_nt(a, b):
    return lax.dot_general(a, b, (((1,), (1,)), ((), ())), preferred_element_type=F32)


def _rope(t, cos, sa, sb):
    up = pltpu.roll(t, V7X_LANES - ROT_DIM // 2, 1)
    dn = pltpu.roll(t, ROT_DIM // 2, 1)
    return t * cos + up * sa + dn * sb


def _inproj_kernel(x_ref, g_ref, w_ref, wdt_ref, z_ref, xbc_ref, dt_ref, *, d_inner, conv_dim):
    xn = _rms(x_ref[...], g_ref[...]).astype(BF16)
    z_ref[...] = _dot(xn, w_ref[:, 0:d_inner])
    xbc_ref[...] = _dot(xn, w_ref[:, d_inner:d_inner + conv_dim])
    dt_ref[...] = _dot(xn, wdt_ref[...])


def ssd_in_proj(x, g, w, w_dt, layer, *, d_inner, conv_dim, tm):
    m, d = x.shape
    dtw = w_dt.shape[-1]
    n = d_inner + conv_dim + dtw
    vm = 2 * tm * d * 4 + d * n * 2 + 2 * tm * n * 4
    return pl.pallas_call(
        functools.partial(_inproj_kernel, d_inner=d_inner, conv_dim=conv_dim),
        grid=(m // tm,),
        in_specs=[pl.BlockSpec((tm, d), lambda i: (i, 0)), _const_spec((1, d)),
                  _layer_spec(w, layer), _layer_spec(w_dt, layer)],
        out_specs=[pl.BlockSpec((tm, d_inner), lambda i: (i, 0)),
                   pl.BlockSpec((tm, conv_dim), lambda i: (i, 0)),
                   pl.BlockSpec((tm, dtw), lambda i: (i, 0))],
        out_shape=[jax.ShapeDtypeStruct((m, d_inner), F32),
                   jax.ShapeDtypeStruct((m, conv_dim), F32),
                   jax.ShapeDtypeStruct((m, dtw), F32)],
        compiler_params=_params(("parallel",), vm),
        name="ssd_in_proj",
    )(x, g, w, w_dt)


def _qproj_kernel(x_ref, g_ref, w_ref, cos_ref, sa_ref, sb_ref, q_ref, *, n_heads, scale):
    xn = _rms(x_ref[...], g_ref[...]).astype(BF16)
    cos, sa, sb = cos_ref[...], sa_ref[...], sb_ref[...]
    t = _dot(xn, w_ref[...])
    for h in range(n_heads):
        sl = slice(h * V7X_LANES, (h + 1) * V7X_LANES)
        q_ref[:, sl] = (_rope(t[:, sl], cos, sa, sb) * scale).astype(BF16)


def attn_q_proj(x, g, w, layer, tables, *, tm, n_pos_blocks):
    m, d = x.shape
    n = w.shape[-1]
    n_heads = n // V7X_LANES
    tab_spec = pl.BlockSpec((tm, V7X_LANES), lambda i: (i % n_pos_blocks, 0))
    vm = 2 * tm * d * 4 + d * n * 2 + 2 * tm * n * 2 + tm * n * 4 + 6 * tm * V7X_LANES * 4
    return pl.pallas_call(
        functools.partial(_qproj_kernel, n_heads=n_heads, scale=ATTN_HEAD_DIM ** -0.5 * math.log2(math.e)),
        grid=(m // tm,),
        in_specs=[pl.BlockSpec((tm, d), lambda i: (i, 0)), _const_spec((1, d)), _layer_spec(w, layer),
                  tab_spec, tab_spec, tab_spec],
        out_specs=pl.BlockSpec((tm, n), lambda i: (i, 0)),
        out_shape=jax.ShapeDtypeStruct((m, n), BF16),
        compiler_params=_params(("parallel",), vm),
        name="attn_q_proj",
    )(x, g, w, *tables)


def _kvproj_kernel(x_ref, g_ref, w_ref, cos_ref, sa_ref, sb_ref, k_ref, v_ref, kb_ref, vt_ref, *, n_heads):
    xn = _rms(x_ref[...], g_ref[...]).astype(BF16)
    cos, sa, sb = cos_ref[...], sa_ref[...], sb_ref[...]
    qk = n_heads * V7X_LANES
    t = _dot(xn, w_ref[:, :qk])
    for h in range(n_heads):
        sl = slice(h * V7X_LANES, (h + 1) * V7X_LANES)
        k = _rope(t[:, sl], cos, sa, sb)
        k_ref[:, sl] = k
        kb_ref[:, sl] = k.astype(BF16)
    v = _dot(xn, w_ref[:, qk:])
    v_ref[...] = v
    tm = v.shape[0]
    ones_row = jnp.where(lax.broadcasted_iota(jnp.int32, (V_ROWS_PAD, tm), 0) == 0, 1.0, 0.0).astype(BF16)
    for h in range(n_heads):
        sl = slice(h * V7X_LANES, (h + 1) * V7X_LANES)
        vt_ref[h, 0:V7X_LANES, :] = v[:, sl].T.astype(BF16)
        vt_ref[h, V7X_LANES:, :] = ones_row


def shared_kv_proj(x, g, w, tables, *, tm, n_pos_blocks):
    m, d = x.shape
    n = w.shape[-1]
    qk = n // 2
    n_heads = qk // V7X_LANES
    seq = tm * n_pos_blocks
    vr = V7X_LANES + V_ROWS_PAD
    tab_spec = pl.BlockSpec((tm, V7X_LANES), lambda i: (i % n_pos_blocks, 0))
    row = lambda i: (i, 0)
    vm = 2 * tm * d * 4 + d * n * 2 + 2 * tm * n * 6 + tm * n * 4 + 6 * tm * V7X_LANES * 4
    return pl.pallas_call(
        functools.partial(_kvproj_kernel, n_heads=n_heads),
        grid=(m // tm,),
        in_specs=[pl.BlockSpec((tm, d), row), _const_spec((1, d)), _layer_spec(w, 0),
                  tab_spec, tab_spec, tab_spec],
        out_specs=[pl.BlockSpec((tm, qk), row)] * 3
                  + [pl.BlockSpec((None, n_heads, vr, tm), lambda i: (i // n_pos_blocks, 0, 0, i % n_pos_blocks))],
        out_shape=[jax.ShapeDtypeStruct((m, qk), F32), jax.ShapeDtypeStruct((m, qk), F32),
                   jax.ShapeDtypeStruct((m, qk), BF16),
                   jax.ShapeDtypeStruct((m // seq, n_heads, vr, seq), BF16)],
        compiler_params=_params(("parallel",), vm),
        name="shared_kv_proj",
    )(x, g, w, *tables)


def _mixer_out_mlp_kernel(a_ref, wo_ref, go_ref, h_ref, g1_ref, wu_ref, wd_ref, g2_ref, o_ref, *, tf):
    y = _dot(a_ref[...].astype(BF16), wo_ref[...])
    h = h_ref[...] + _rms(y, go_ref[...])
    xn = _rms(h, g1_ref[...]).astype(BF16)
    d_ff = wu_ref.shape[1]
    acc = jnp.zeros(h.shape, F32)
    for f in range(d_ff // tf):
        u = jnp.maximum(_dot(xn, wu_ref[:, f * tf:(f + 1) * tf]), 0.0)
        acc = acc + _dot((u * u).astype(BF16), wd_ref[f * tf:(f + 1) * tf, :])
    o_ref[...] = h + _rms(acc, g2_ref[...])


def mixer_out_mlp(a, w_o, o_layer, g_o, h, g_pre, w_up, w_down, layer, g_post, *, tm, tf=512):
    m, k = a.shape
    d = h.shape[1]
    d_ff = w_up.shape[-1]
    row = lambda i: (i, 0)
    vm = (2 * tm * k * a.dtype.itemsize + k * d * 2 + 4 * tm * d * 4 + 2 * d * d_ff * 2
          + 3 * tm * d * 4 + 2 * tm * tf * 4)
    return pl.pallas_call(
        functools.partial(_mixer_out_mlp_kernel, tf=tf),
        grid=(m // tm,),
        in_specs=[pl.BlockSpec((tm, k), row), _layer_spec(w_o, o_layer), _const_spec((1, d)),
                  pl.BlockSpec((tm, d), row), _const_spec((1, d)), _layer_spec(w_up, layer),
                  _layer_spec(w_down, layer), _const_spec((1, d))],
        out_specs=pl.BlockSpec((tm, d), row),
        out_shape=jax.ShapeDtypeStruct((m, d), F32),
        compiler_params=_params(("parallel",), vm),
        name="mixer_out_mlp",
    )(a, w_o, g_o, h, g_pre, w_up, w_down, g_post)


def _expand3(v, e):
    hi = v.astype(BF16)
    r1 = v - hi.astype(F32)
    mid = r1.astype(BF16)
    lo = (r1 - mid.astype(F32)).astype(BF16)
    return _dot(hi, e) + _dot(mid, e) + _dot(lo, e)


def _gated_group_norm(y, z, gate_norm, n_groups):
    g = y * _silu(z)
    width = g.shape[1] // n_groups
    outs = []
    for i in range(n_groups):
        gg = g[:, i * width:(i + 1) * width]
        ms = jnp.mean(gg * gg, axis=-1, keepdims=True)
        outs.append(gg * lax.rsqrt(ms + RMS_EPS) * gate_norm[:, i * width:(i + 1) * width])
    return outs


def _ssd_prompt_kernel(z_ref, xbc_ref, dtr_ref, cw_ref, cb_ref, dtb_ref, alog_ref, dexp_ref, gn_ref,
                       tri_ref, e64_ref,
                       g_ref, ssm_ref, conv_ref,
                       cbuf, st_t, y_s, *, n_heads, d_inner):
    L = SSD_CHUNK
    N = SSD_D_STATE
    gn_w = SSD_N_GROUPS * N
    heads_per_group = n_heads // SSD_N_GROUPS
    c = pl.program_id(1)
    last = pl.num_programs(1) - 1
    halo = V7X_SUBLANES

    @pl.when(c == 0)
    def _():
        cbuf[0:halo, :] = jnp.zeros((halo, cbuf.shape[1]), F32)
        st_t[...] = jnp.zeros(st_t.shape, F32)

    cbuf[halo:halo + L, :] = xbc_ref[...]
    xw = cbuf[...]
    acc = cb_ref[...] + xw[halo:, :] * cw_ref[SSD_D_CONV - 1:SSD_D_CONV, :]
    for k in range(1, SSD_D_CONV):
        acc = acc + pltpu.roll(xw, k, 0)[halo:, :] * cw_ref[SSD_D_CONV - 1 - k:SSD_D_CONV - k, :]
    act = _silu(acc)
    x = act[:, :d_inner]
    cbuf[0:halo, :] = xw[L:, :]

    dt = _softplus(dtr_ref[...] + dtb_ref[...])
    a = -jnp.exp(alog_ref[...]) * math.log2(math.e)
    acum = jnp.dot(tri_ref[...], a * dt, precision=lax.Precision.HIGHEST, preferred_element_type=F32)
    acum_last = acum[L - 1:L, :]
    acum_t = acum.T
    cd = _expand3(jnp.broadcast_to(jnp.exp2(acum_last), (V7X_SUBLANES, V7X_LANES)), e64_ref[...])[0:1, :]
    e64 = e64_ref[...]
    dt_x = _dot(dt.astype(BF16), e64)
    w_x = _dot((jnp.exp2(acum_last - acum) * dt).astype(BF16), e64)
    e1_x = _dot(jnp.exp2(acum).astype(BF16), e64)
    xdt = (x * dt_x).astype(BF16)
    xs = (x * w_x).astype(BF16)

    li = lax.broadcasted_iota(jnp.int32, (L, L), 0)
    si = lax.broadcasted_iota(jnp.int32, (L, L), 1)
    causal = li >= si
    lo_half = lax.broadcasted_iota(jnp.int32, (L, V7X_LANES), 1) < SSD_HEAD_DIM

    for grp in range(SSD_N_GROUPS):
        b_g = act[:, d_inner + grp * N:d_inner + (grp + 1) * N]
        c_g = act[:, d_inner + gn_w + grp * N:d_inner + gn_w + (grp + 1) * N].astype(BF16)
        cb_g = jnp.where(causal, _dot_nt(c_g, b_g.astype(BF16)), 0.0)
        bt_g = b_g.T.astype(BF16)
        for pr in range(heads_per_group // 2):
            q = grp * (heads_per_group // 2) + pr
            sl = slice(q * V7X_LANES, (q + 1) * V7X_LANES)
            st_prev = st_t[:, sl]
            y_off = _dot(c_g, st_prev.astype(BF16)) * e1_x[:, sl]
            st_t[:, sl] = st_prev * cd[:, sl] + _dot(bt_g, xs[:, sl])
            xp = xdt[:, sl]
            ys = []
            for h in (2 * q, 2 * q + 1):
                seg = jnp.minimum(jnp.broadcast_to(acum[:, h:h + 1], (L, L)) - acum_t[h:h + 1, :], 0.0)
                ys.append(_dot((jnp.exp2(seg) * cb_g).astype(BF16), xp))
            y_s[:, sl] = jnp.where(lo_half, ys[0], ys[1]) + y_off

    y = y_s[...] + x * dexp_ref[...]
    outs = _gated_group_norm(y, z_ref[...], gn_ref[...], SSD_N_GROUPS)
    width = d_inner // SSD_N_GROUPS
    for i, o in enumerate(outs):
        g_ref[:, i * width:(i + 1) * width] = o.astype(BF16)

    @pl.when(c == last)
    def _():
        conv_ref[...] = cbuf[0:halo, :]
        for q in range(n_heads // 2):
            sl = slice(q * V7X_LANES, (q + 1) * V7X_LANES)
            ssm_ref[sl, :] = st_t[:, sl].T


def ssd_prompt(z, xbc, dtr, cw, cb, dtb, alog, dexp, gn, consts, *, bsz, seqlen, n_heads):
    m, d_inner = z.shape
    conv_dim = xbc.shape[1]
    L = SSD_CHUNK
    nc = seqlen // L
    tri, e64 = consts
    blk = lambda b, c: (b * nc + c, 0)
    vm = (2 * L * (d_inner + conv_dim + V7X_LANES) * 4 + 2 * L * d_inner * 2
          + (L + 8) * conv_dim * 4 + 2 * L * d_inner * 4
          + 2 * e64.size * 2 + 4 * n_heads * SSD_HEAD_DIM * SSD_D_STATE * 4
          + 8 * L * conv_dim * 4)
    return pl.pallas_call(
        functools.partial(_ssd_prompt_kernel, n_heads=n_heads, d_inner=d_inner),
        grid=(bsz, nc),
        in_specs=[pl.BlockSpec((L, d_inner), blk), pl.BlockSpec((L, conv_dim), blk),
                  pl.BlockSpec((L, V7X_LANES), blk),
                  _const_spec(cw.shape), _const_spec(cb.shape), _const_spec(dtb.shape),
                  _const_spec(alog.shape), _const_spec(dexp.shape), _const_spec(gn.shape),
                  _const_spec(tri.shape), _const_spec(e64.shape)],
        out_specs=[pl.BlockSpec((L, d_inner), blk),
                   pl.BlockSpec((None, n_heads * SSD_HEAD_DIM, SSD_D_STATE), lambda b, c: (b, 0, 0)),
                   pl.BlockSpec((None, V7X_SUBLANES, conv_dim), lambda b, c: (b, 0, 0))],
        out_shape=[jax.ShapeDtypeStruct((m, d_inner), BF16),
                   jax.ShapeDtypeStruct((bsz, n_heads * SSD_HEAD_DIM, SSD_D_STATE), F32),
                   jax.ShapeDtypeStruct((bsz, V7X_SUBLANES, conv_dim), F32)],
        scratch_shapes=[pltpu.VMEM((L + V7X_SUBLANES, conv_dim), F32),
                        pltpu.VMEM((SSD_D_STATE, d_inner), F32),
                        pltpu.VMEM((L, d_inner), F32)],
        compiler_params=_params(("arbitrary", "arbitrary"), vm),
        name="ssd_prompt",
    )(z, xbc, dtr, cw, cb, dtb, alog, dexp, gn, tri, e64)


def _ssd_sample_conv_kernel(xbc_ref, cs_ref, dtr_ref, cw_ref, cb_ref, dtb_ref,
                            x_ref, b_ref, c_ref, dt_ref, ncs_ref, *, d_inner):
    new = xbc_ref[...]
    acc = cb_ref[...] + new * cw_ref[SSD_D_CONV - 1:SSD_D_CONV, :]
    for k in range(SSD_D_CONV - 1):
        acc = acc + cs_ref[k] * cw_ref[k:k + 1, :]
    act = _silu(acc)
    gn_w = SSD_N_GROUPS * SSD_D_STATE
    x_ref[...] = act[:, :d_inner]
    b_ref[...] = act[:, d_inner:d_inner + gn_w]
    c_ref[...] = act[:, d_inner + gn_w:]
    dt_ref[...] = _softplus(dtr_ref[...] + dtb_ref[...])
    for k in range(SSD_D_CONV - 2):
        ncs_ref[k] = cs_ref[k + 1]
    ncs_ref[SSD_D_CONV - 2] = new


def ssd_sample_conv(xbc, cs_t, dtr, cw, cb, dtb, *, d_inner):
    n, conv_dim = xbc.shape
    gn_w = SSD_N_GROUPS * SSD_D_STATE
    args = (xbc, cs_t, dtr, cw, cb, dtb)
    vm = 6 * xbc.size * 4 + 4 * cs_t.size * 4
    return pl.pallas_call(
        functools.partial(_ssd_sample_conv_kernel, d_inner=d_inner),
        grid=(1,),
        in_specs=[_whole_spec(a.shape) for a in args],
        out_specs=[_whole_spec((n, d_inner)), _whole_spec((n, gn_w)), _whole_spec((n, gn_w)),
                   _whole_spec(dtr.shape), _whole_spec(cs_t.shape)],
        out_shape=[jax.ShapeDtypeStruct((n, d_inner), F32), jax.ShapeDtypeStruct((n, gn_w), F32),
                   jax.ShapeDtypeStruct((n, gn_w), F32), jax.ShapeDtypeStruct(dtr.shape, F32),
                   jax.ShapeDtypeStruct(cs_t.shape, F32)],
        compiler_params=_params(("arbitrary",), vm),
        name="ssd_sample_conv",
    )(*args)


def _ssd_sample_state_kernel(st_ref, xt_ref, dtb_ref, dtr_ref, alog_ref, b_ref, c_ref, *rest, sb, n_heads):
    nst_ref, y_ref = rest[-2:]
    heads_per_group = n_heads // SSD_N_GROUPS
    N = SSD_D_STATE
    P = SSD_HEAD_DIM
    a_rep = -jnp.exp(alog_ref[...])
    for s in range(sb):
        dec = jnp.exp(a_rep * dtb_ref[s])
        xdt_t = xt_ref[s] * dtr_ref[s]
        for grp in range(SSD_N_GROUPS):
            b_row = b_ref[s][:, grp * N:(grp + 1) * N]
            c_row = c_ref[s][:, grp * N:(grp + 1) * N]
            news = []
            for r in range(heads_per_group):
                h = grp * heads_per_group + r
                new = st_ref[s, h] * dec[h:h + 1, :] + xdt_t[:, h:h + 1] * b_row
                nst_ref[s, h] = new
                news.append(new)
            hg = jnp.concatenate(news, axis=0).astype(BF16)
            c8 = jnp.broadcast_to(c_row, (V7X_SUBLANES, N)).astype(BF16)
            yg = _dot_nt(c8, hg)
            y_ref[s, :, grp * heads_per_group * P:(grp + 1) * heads_per_group * P] = yg[0:1, :]


def ssd_sample_state(states, layer, stacked, x_t, dt_bcast, dt_row, alog_rep, b_in, c_in, *, sb):
    _, n, n_heads, P, N = states.shape
    d_inner = n_heads * P
    gn_w = b_in.shape[-1]
    lyr = lambda i: (layer, i, 0, 0, 0)
    blk3 = lambda i: (i, 0, 0)
    vm = 4 * sb * n_heads * P * N * 4 + 4 * sb * (P * V7X_LANES + n_heads * V7X_LANES) * 4
    args = [states, x_t, dt_bcast, dt_row, alog_rep, b_in, c_in]
    in_specs = [pl.BlockSpec((None, sb, n_heads, P, N), lyr),
                pl.BlockSpec((sb, P, n_heads), blk3),
                pl.BlockSpec((sb, n_heads, V7X_LANES), blk3),
                pl.BlockSpec((sb, 1, n_heads), blk3),
                _const_spec(alog_rep.shape),
                pl.BlockSpec((sb, 1, gn_w), blk3),
                pl.BlockSpec((sb, 1, gn_w), blk3)]
    aliases = {}
    if stacked is not None:
        aliases = {len(args): 0}
        args.append(stacked)
        in_specs.append(pl.BlockSpec(memory_space=pl.ANY))
    return pl.pallas_call(
        functools.partial(_ssd_sample_state_kernel, sb=sb, n_heads=n_heads),
        grid=(n // sb,),
        in_specs=in_specs,
        out_specs=[pl.BlockSpec((None, sb, n_heads, P, N), lyr),
                   pl.BlockSpec((sb, 1, d_inner), blk3)],
        out_shape=[jax.ShapeDtypeStruct(states.shape, F32),
                   jax.ShapeDtypeStruct((n, 1, d_inner), F32)],
        input_output_aliases=aliases,
        compiler_params=_params(("parallel",), vm),
        name="ssd_sample_state",
    )(*args)


def _ssd_sample_gate_kernel(y_ref, x_ref, z_ref, dexp_ref, gn_ref, g_ref):
    y = y_ref[...] + x_ref[...] * dexp_ref[...]
    outs = _gated_group_norm(y, z_ref[...], gn_ref[...], SSD_N_GROUPS)
    width = y.shape[1] // SSD_N_GROUPS
    for i, o in enumerate(outs):
        g_ref[:, i * width:(i + 1) * width] = o.astype(BF16)


def ssd_sample_gate(y, x, z, dexp, gn):
    args = (y, x, z, dexp, gn)
    return pl.pallas_call(
        _ssd_sample_gate_kernel,
        grid=(1,),
        in_specs=[_whole_spec(a.shape) for a in args],
        out_specs=_whole_spec(y.shape),
        out_shape=jax.ShapeDtypeStruct(y.shape, BF16),
        compiler_params=_params(("arbitrary",), 8 * y.size * 4),
        name="ssd_sample_gate",
    )(*args)


def _diff_lambda(lam_ref, lam_init):
    v = lam_ref[...]
    d1 = jnp.sum(v[0:1, :] * v[1:2, :], axis=-1, keepdims=True)
    d2 = jnp.sum(v[2:3, :] * v[3:4, :], axis=-1, keepdims=True)
    return jnp.exp(d1) - jnp.exp(d2) + lam_init


def _flash_body(lam_ref, q_ref, k_ref, vt_ref, subln_ref, o_ref, scratch, *, t, hb, unroll, lam_init, tail_work):
    qi = pl.program_id(2)
    m_s, acc_s = scratch[:hb], scratch[hb:]
    q2 = []
    for s in range(hb):
        q = q_ref[:, s * V7X_LANES:(s + 1) * V7X_LANES]
        lane = lax.broadcasted_iota(jnp.int32, q.shape, 1)
        zero = jnp.zeros_like(q)
        q2.append(jnp.concatenate([jnp.where(lane < ATTN_HEAD_DIM, q, zero),
                                   jnp.where(lane >= ATTN_HEAD_DIM, q, zero)], axis=0))
        m_s[s][...] = jnp.full(m_s[s].shape, NEG_BIG, F32)
        acc_s[s][...] = jnp.zeros(acc_s[s].shape, F32)

    def accumulate(j, scs, masked):
        start = pl.multiple_of(j * t, t)
        for s in range(hb):
            sc = scs[s]
            if masked:
                kv = lax.broadcasted_iota(jnp.int32, sc.shape, 0)
                qq = lax.broadcasted_iota(jnp.int32, sc.shape, 1)
                qq = jnp.where(qq >= t, qq - t, qq)
                sc = jnp.where(kv <= qq, sc, NEG_BIG)
            m_old = m_s[s][...]
            m_new = jnp.maximum(m_old, jnp.max(sc, axis=0, keepdims=True))
            alpha = jnp.exp2(m_old - m_new)
            p = jnp.exp2(sc - m_new).astype(BF16)
            acc_s[s][...] = alpha * acc_s[s][...] + _dot(vt_ref[s, :, pl.ds(start, t)], p)
            m_s[s][...] = m_new

    def run_blocks(j0, n, masked_last):
        start = pl.multiple_of(j0 * t, t)
        wide = [_dot_nt(k_ref[pl.ds(start, n * t), s * V7X_LANES:(s + 1) * V7X_LANES], q2[s])
                for s in range(hb)]
        for u in range(n):
            accumulate(j0 + u, [w[u * t:(u + 1) * t, :] for w in wide], masked_last and u == n - 1)

    def body(g, carry):
        run_blocks(g * unroll, unroll, False)
        return carry

    n_groups = qi // unroll
    lax.fori_loop(0, n_groups, body, 0)
    rem = qi - n_groups * unroll
    for r in range(unroll):
        @pl.when(rem == r)
        def _(r=r):
            run_blocks(n_groups * unroll, r + 1, True)
            tail_work()

    lam = _diff_lambda(lam_ref, lam_init)
    gain = subln_ref[...] * (1.0 - lam_init)
    for s in range(hb):
        acc = acc_s[s][...]
        acc = acc[:V7X_LANES, :] * (1.0 / acc[V7X_LANES:V7X_LANES + 1, :])
        o_t = acc[:, :t] - lam * acc[:, t:]
        ms = jnp.mean(o_t * o_t, axis=0, keepdims=True)
        o_t = o_t * lax.rsqrt(ms + RMS_EPS)
        o_ref[:, s * V7X_LANES:(s + 1) * V7X_LANES] = (o_t.T * gain).astype(BF16)


def _decode_body(lam_ref, q_ref, kn_ref, vn_ref, subln_ref, k_refs, v_refs, o_ref, s, *, lam_init):
    n_pages = len(k_refs)
    _, page, n_heads, hd = k_refs[0].shape
    rows = 2 * n_heads
    prow = page * n_heads
    q8 = q_ref[s].astype(F32)
    lane = lax.broadcasted_iota(jnp.int32, q8.shape, 1)
    q16 = jnp.concatenate([jnp.where(lane < ATTN_HEAD_DIM, q8, 0.0),
                           jnp.where(lane >= ATTN_HEAD_DIM, q8, 0.0)], axis=0)
    q16b = q16.astype(BF16)
    r_i = lax.broadcasted_iota(jnp.int32, (rows, prow), 0)
    c_i = lax.broadcasted_iota(jnp.int32, (rows, prow), 1)
    own = (r_i % n_heads) == (c_i % n_heads)

    s_parts = []
    for p in range(n_pages):
        kp = k_refs[p][0].reshape(prow, hd).astype(BF16)
        s_parts.append(jnp.where(own, _dot_nt(q16b, kp), NEG_BIG))
    kn = kn_ref[s]
    kn2 = jnp.concatenate([kn, kn], axis=0)
    s_new = jnp.sum(q16 * kn2, axis=1, keepdims=True)
    m = s_new
    for sp in s_parts:
        m = jnp.maximum(m, jnp.max(sp, axis=1, keepdims=True))
    p_new = jnp.exp2(s_new - m)
    vn = vn_ref[s]
    acc = p_new * jnp.concatenate([vn, vn], axis=0)
    denom = p_new
    for p in range(n_pages):
        pp = jnp.exp2(s_parts[p] - m)
        denom = denom + jnp.sum(pp, axis=1, keepdims=True)
        acc = acc + _dot(pp.astype(BF16), v_refs[p][0].reshape(prow, hd).astype(BF16))

    lam = _diff_lambda(lam_ref, lam_init)
    acc = acc * (1.0 / denom)
    d = acc[:n_heads, :] - lam * acc[n_heads:, :]
    ms = jnp.mean(d * d, axis=1, keepdims=True)
    o_ref[s] = d * lax.rsqrt(ms + RMS_EPS) * (subln_ref[...] * (1.0 - lam_init))


def _attn_kernel(pt_ref, lam_ref, q_ref, k_ref, vt_ref, subln_ref, qd_ref, kn_ref, vn_ref, *rest,
                 n_pages, spp, t, hb, unroll, lam_init):
    del pt_ref
    n_pg = spp * n_pages
    k_refs, v_refs = rest[:n_pg], rest[n_pg:2 * n_pg]
    o_ref, od_ref = rest[2 * n_pg], rest[2 * n_pg + 1]
    scratch = rest[2 * n_pg + 2:]
    def decode():
        for s in range(spp):
            _decode_body(lam_ref, qd_ref, kn_ref, vn_ref, subln_ref,
                         k_refs[s * n_pages:(s + 1) * n_pages], v_refs[s * n_pages:(s + 1) * n_pages],
                         od_ref, s, lam_init=lam_init)

    _flash_body(lam_ref, q_ref, k_ref, vt_ref, subln_ref, o_ref, scratch,
                t=t, hb=hb, unroll=unroll, lam_init=lam_init, tail_work=decode)


def diff_attn(page_table, lam_vecs, q, k, v_t, subln, q_dec, k_new, v_new, cache_k, cache_v, *,
              bsz, seqlen, n_heads, lam_init, t=256, hb=4, unroll=2):
    nq = seqlen // t
    n_hg = n_heads // hb
    n_steps = bsz * n_hg * nq
    n, n_pages = page_table.shape
    spp, ragged = divmod(n, n_steps)
    assert spp >= 1 and ragged == 0, "sample sequences must tile the prompt attention grid"
    _, page, _, hd = cache_k.shape
    vr = v_t.shape[2]
    w = hb * V7X_LANES
    step = lambda b, h, i: (b * n_hg + h) * nq + i
    qblk = lambda b, h, i, pt: (b * nq + i, h)
    dec = lambda b, h, i, pt: (step(b, h, i), 0, 0)
    const2 = lambda b, h, i, pt: (0, 0)
    page_specs = [pl.BlockSpec((1, page, n_heads, hd),
                               functools.partial(lambda b, h, i, pt, s, p: (pt[step(b, h, i) * spp + s, p], 0, 0, 0),
                                                 s=s, p=p))
                  for s in range(spp) for p in range(n_pages)]
    resident = dict(pipeline_mode=pl.Buffered(1))
    vm = (2 * 2 * t * w * 2 + seqlen * w * 2 + hb * vr * seqlen * 2 + hb * (vr + 8) * 2 * t * 4
          + hb * 4 * t * 2 * t * 4
          + 2 * 2 * spp * n_pages * page * n_heads * hd * 4 + 4 * 2 * n_heads * page * n_heads * n_pages * 4)
    grid_spec = pltpu.PrefetchScalarGridSpec(
        num_scalar_prefetch=1,
        grid=(bsz, n_hg, nq),
        in_specs=[pl.BlockSpec(lam_vecs.shape, const2),
                  pl.BlockSpec((t, w), qblk),
                  pl.BlockSpec((seqlen, w), lambda b, h, i, pt: (b, h), **resident),
                  pl.BlockSpec((None, hb, vr, seqlen), lambda b, h, i, pt: (b, h, 0, 0), **resident),
                  pl.BlockSpec(subln.shape, const2),
                  pl.BlockSpec((spp, n_heads, hd), dec), pl.BlockSpec((spp, n_heads, hd), dec),
                  pl.BlockSpec((spp, n_heads, hd), dec)]
                 + page_specs + page_specs,
        out_specs=[pl.BlockSpec((t, w), qblk), pl.BlockSpec((spp, n_heads, hd), dec)],
        scratch_shapes=[pltpu.VMEM((1, 2 * t), F32)] * hb + [pltpu.VMEM((vr, 2 * t), F32)] * hb,
    )
    return pl.pallas_call(
        functools.partial(_attn_kernel, n_pages=n_pages, spp=spp, t=t, hb=hb, unroll=unroll, lam_init=lam_init),
        grid_spec=grid_spec,
        out_shape=[jax.ShapeDtypeStruct(q.shape, BF16), jax.ShapeDtypeStruct((n, n_heads, hd), F32)],
        compiler_params=_params(("parallel", "parallel", "arbitrary"), vm),
        name="diff_attn",
    )(page_table, lam_vecs, q, k, v_t, subln, q_dec, k_new, v_new,
      *([cache_k] * (spp * n_pages)), *([cache_v] * (spp * n_pages)))


def _rope_tables(pos):
    half = ROT_DIM // 2
    inv = ROPE_THETA ** (-jnp.arange(0, ROT_DIM, 2, dtype=F32) / ROT_DIM)
    ang = pos.astype(F32)[:, None] * inv[None, :]
    cos, sin = jnp.cos(ang), jnp.sin(ang)
    n = pos.shape[0]
    pad = jnp.zeros((n, ATTN_HEAD_DIM - ROT_DIM), F32)
    zeros = jnp.zeros((n, half), F32)
    c_map = jnp.concatenate([cos, cos, pad + 1.0], axis=1)
    sa_map = jnp.concatenate([-sin, zeros, pad], axis=1)
    sb_map = jnp.concatenate([zeros, sin, pad], axis=1)
    return tuple(jnp.concatenate([t, t], axis=1) for t in (c_map, sa_map, sb_map))


def _row(v, width=None):
    v = v.astype(F32).reshape(1, -1)
    if width is not None and v.shape[1] < width:
        v = jnp.pad(v, ((0, 0), (0, width - v.shape[1])))
    return v


def kernel(x_prompt, x_sample, state_ssm, state_conv, cache_k, cache_v, page_table, ssd_norm_pre, ssd_norm_post, ssd_w_in, ssd_conv_w, ssd_conv_b, ssd_dt_bias, ssd_a_log, ssd_d, ssd_gate_norm, ssd_w_out, mlp_norm_pre, mlp_norm_post, mlp_w_up, mlp_w_down, kv_norm, w_kv, attn_norm_pre, attn_norm_post, attn_w_q, attn_lambda_q1, attn_lambda_k1, attn_lambda_q2, attn_lambda_k2, attn_subln, attn_w_o):
    bsz, seqlen, d_model = x_prompt.shape
    n_dec = x_sample.shape[0]
    n_ssd = ssd_w_in.shape[0]
    depth = mlp_w_up.shape[0]
    n_heads_ssd = ssd_a_log.shape[1]
    d_inner = n_heads_ssd * SSD_HEAD_DIM
    conv_dim = ssd_conv_w.shape[2]
    qk_dim = attn_w_q.shape[2]
    n_heads = qk_dim // V7X_LANES
    m_p = bsz * seqlen

    hp = x_prompt.reshape(m_p, d_model)
    hs = x_sample.reshape(n_dec, d_model)

    tm_p = 512
    tm_mlp = 512
    tm_s = n_dec

    L = SSD_CHUNK
    tri = (jnp.arange(L)[:, None] >= jnp.arange(L)[None, :]).astype(F32)
    hrow = jnp.arange(V7X_LANES)[:, None]
    e64 = (hrow == (jnp.arange(d_inner)[None, :] // SSD_HEAD_DIM)).astype(BF16)

    tab_p = _rope_tables(jnp.arange(seqlen))
    past_len = page_table.shape[1] * cache_k.shape[1]
    tab_s = tuple(jnp.broadcast_to(t, (n_dec, V7X_LANES)) for t in _rope_tables(jnp.full((1,), past_len)))

    ssm_p, conv_p, conv_s = [], [], []
    ssm_s = None
    k_p = v_p = k_s = v_s = None
    kb_p = vt_p = None

    w_in_b = ssd_w_in.astype(BF16)
    w_dt_b = jnp.pad(w_in_b[:, :, d_inner + conv_dim:],
                     ((0, 0), (0, 0), (0, V7X_LANES - n_heads_ssd)))
    w_out_b = ssd_w_out.astype(BF16)
    w_q_b = attn_w_q.astype(BF16)
    w_o_b = attn_w_o.astype(BF16)
    w_kv_b = w_kv.astype(BF16)[None]
    wu = mlp_w_up.astype(BF16)
    wd = mlp_w_down.astype(BF16)

    for i in range(depth):
        if i < n_ssd:
            j = i
            g_pre = _row(ssd_norm_pre[j])
            g_post = _row(ssd_norm_post[j])
            cw = ssd_conv_w[j].astype(F32)
            cb = _row(ssd_conv_b[j])
            dtb = _row(ssd_dt_bias[j], V7X_LANES)
            alog = _row(ssd_a_log[j], V7X_LANES)
            dexp = jnp.repeat(ssd_d[j].astype(F32), SSD_HEAD_DIM).reshape(1, d_inner)
            gn = _row(ssd_gate_norm[j])

            z, xbc, dtr = ssd_in_proj(hp, g_pre, w_in_b, w_dt_b, j, d_inner=d_inner, conv_dim=conv_dim, tm=tm_p)
            g, sp, cp = ssd_prompt(z, xbc, dtr, cw, cb, dtb, alog, dexp, gn, (tri, e64),
                                   bsz=bsz, seqlen=seqlen, n_heads=n_heads_ssd)
            mix_p, mix_s, w_mix = g, None, w_out_b
            ssm_p.append(sp.reshape(bsz, n_heads_ssd, SSD_HEAD_DIM, SSD_D_STATE))
            conv_p.append(cp[:, V7X_SUBLANES - (SSD_D_CONV - 1):, :])

            zs, xbcs, dtrs = ssd_in_proj(hs, g_pre, w_in_b, w_dt_b, j, d_inner=d_inner, conv_dim=conv_dim, tm=tm_s)
            cs_t = jnp.transpose(state_conv[j].astype(F32), (1, 0, 2))
            xs, bs_, cs_, dts, ncs = ssd_sample_conv(xbcs, cs_t, dtrs, cw, cb, dtb, d_inner=d_inner)
            x_t = jnp.transpose(xs.reshape(n_dec, n_heads_ssd, SSD_HEAD_DIM), (0, 2, 1))
            dt_h = dts[:, :n_heads_ssd]
            dt_bcast = jnp.broadcast_to(dt_h[:, :, None], (n_dec, n_heads_ssd, V7X_LANES))
            alog_rep = jnp.broadcast_to(ssd_a_log[j].astype(F32)[:, None], (n_heads_ssd, V7X_LANES))
            ssm_s, ys = ssd_sample_state(state_ssm, j, ssm_s, x_t, dt_bcast, dt_h[:, None, :], alog_rep,
                                         bs_[:, None, :], cs_[:, None, :], sb=4)
            mix_s = ssd_sample_gate(ys.reshape(n_dec, d_inner), xs, zs, dexp, gn)
            conv_s.append(jnp.transpose(ncs, (1, 0, 2)).astype(state_conv.dtype))
        else:
            j = i - n_ssd
            if j == 0:
                g_kv = _row(kv_norm)
                k_p, v_p, kb_p, vt_p = shared_kv_proj(hp, g_kv, w_kv_b, tab_p, tm=tm_p, n_pos_blocks=seqlen // tm_p)
                k_s, v_s, _, _ = shared_kv_proj(hs, g_kv, w_kv_b, tab_s, tm=tm_s, n_pos_blocks=1)
            lam_init = 0.8 - 0.6 * math.exp(-0.3 * i)
            lam_vecs = jnp.pad(
                jnp.stack([attn_lambda_q1[j], attn_lambda_k1[j], attn_lambda_q2[j], attn_lambda_k2[j]]).astype(F32),
                ((0, V7X_SUBLANES - 4), (0, V7X_LANES - ATTN_HEAD_DIM)))
            g_pre = _row(attn_norm_pre[j])
            g_post = _row(attn_norm_post[j])
            subln = _row(attn_subln[j])

            qp = attn_q_proj(hp, g_pre, w_q_b, j, tab_p, tm=tm_p, n_pos_blocks=seqlen // tm_p)
            qs = attn_q_proj(hs, g_pre, w_q_b, j, tab_s, tm=tm_s, n_pos_blocks=1)
            heads = lambda a: a.reshape(n_dec, n_heads, V7X_LANES)
            mix_p, os_ = diff_attn(page_table, lam_vecs, qp, kb_p, vt_p, subln, heads(qs), heads(k_s), heads(v_s),
                                   cache_k, cache_v, bsz=bsz, seqlen=seqlen, n_heads=n_heads, lam_init=lam_init)
            mix_s, w_mix = os_.reshape(n_dec, qk_dim), w_o_b

        g1 = _row(mlp_norm_pre[i])
        g2 = _row(mlp_norm_post[i])
        hp = mixer_out_mlp(mix_p, w_mix, j, g_post, hp, g1, wu, wd, i, g2, tm=tm_mlp)
        hs = mixer_out_mlp(mix_s, w_mix, j, g_post, hs, g1, wu, wd, i, g2, tm=tm_s)

    y_prompt = hp.reshape(bsz, seqlen, d_model)
    y_sample = hs.reshape(n_dec, 1, d_model)
    p_k = k_p.reshape(bsz, seqlen, n_heads, V7X_LANES)
    p_v = v_p.reshape(bsz, seqlen, n_heads, V7X_LANES)
    s_k = k_s.reshape(n_dec, 1, n_heads, V7X_LANES)
    s_v = v_s.reshape(n_dec, 1, n_heads, V7X_LANES)
    return (y_prompt, y_sample, jnp.stack(ssm_p), jnp.stack(conv_p), p_k, p_v,
            ssm_s.astype(state_ssm.dtype), jnp.stack(conv_s), s_k, s_v)
```
